```python
import math
import jax, jax.numpy as jnp
from jax import lax
import numpy as np

D_MODEL = 2048
BATCH = 1
SEQ = 16384
DEPTH = 2

GRID_W = 64
CTX_LEN = 256
HEAD_DIM = 128
ROPE_THETA = 10000.0
Q_BLOCK = 128
EPS = 1e-6
D_MIX = D_MODEL
A_HEADS = D_MIX // 2 // HEAD_DIM
A_KV_HEADS = 2
C_CONV = D_MIX // 2
CONV_W = 31
IN_EVEN = A_HEADS * HEAD_DIM + 2 * A_KV_HEADS * HEAD_DIM + 2 * C_CONV
DIFF_HEADS = D_MIX // (2 * HEAD_DIM)
IN_ODD = 3 * D_MIX
LAMBDA_STD = 0.1
N_EXPERTS = 64
TOP_K = 8
N_GROUPS = 8
TOPK_GROUPS = 4
EXPERT_FF = D_MODEL // 4
ROUTED_SCALE = 2.5
MOE_BLOCK = 128

N_EVEN = (DEPTH + 1) // 2
N_ODD = DEPTH // 2

kernel_name = 'hybrid_diffusion_gqa_conformer_diffattn_moe'


def rms_norm(x, g):
    xf = x.astype(jnp.float32)
    y = xf * lax.rsqrt(jnp.mean(xf * xf, axis=-1, keepdims=True) + EPS)
    return (y * g.astype(jnp.float32)).astype(x.dtype)


def layer_norm(x, g, b):
    xf = x.astype(jnp.float32)
    mu = jnp.mean(xf, axis=-1, keepdims=True)
    var = jnp.mean(jnp.square(xf - mu), axis=-1, keepdims=True)
    y = (xf - mu) * lax.rsqrt(var + EPS)
    return (y * g.astype(jnp.float32) + b.astype(jnp.float32)).astype(x.dtype)


def modulate(x, g, shift, scale):
    return rms_norm(x, g) * (1 + scale) + shift


def axial_rope_tables(n_tokens, dtype):
    rows = n_tokens // GRID_W
    nf = HEAD_DIM // 4
    inv = ROPE_THETA ** (-jnp.arange(nf, dtype=jnp.float32) / nf)
    row_ang = jnp.arange(rows, dtype=jnp.float32)[:, None] * inv
    col_ang = jnp.arange(GRID_W, dtype=jnp.float32)[:, None] * inv
    ang = jnp.stack([jnp.broadcast_to(row_ang[:, None, :], (rows, GRID_W, nf)),
                     jnp.broadcast_to(col_ang[None, :, :], (rows, GRID_W, nf))], axis=2)
    ang = ang.reshape(rows * GRID_W, 2, nf)
    return jnp.cos(ang).astype(dtype), jnp.sin(ang).astype(dtype)


def apply_rope(x, cos, sin):
    b, l, h, d = x.shape
    xr = x.reshape(b, l, h, 2, 2, d // 4)
    x1, x2 = xr[..., 0, :], xr[..., 1, :]
    cs, sn = cos[None, :, None], sin[None, :, None]
    out = jnp.stack([x1 * cs - x2 * sn, x2 * cs + x1 * sn], axis=-2)
    return out.reshape(b, l, h, d)


def to_blocks(q):
    b, l = q.shape[:2]
    q = q.reshape((b, l // Q_BLOCK, Q_BLOCK) + q.shape[2:])
    return jnp.moveaxis(q, 1, 0)


def from_blocks(o):
    o = jnp.moveaxis(o, 0, 1)
    return o.reshape((o.shape[0], o.shape[1] * o.shape[2]) + o.shape[3:])


def gqa_attention(q, k, v):
    b, lq = q.shape[:2]
    g = A_HEADS // A_KV_HEADS
    qb = to_blocks(q.reshape(b, lq, A_KV_HEADS, g, HEAD_DIM))
    scale = HEAD_DIM ** -0.5

    def block(qi):
        s = jnp.einsum('bqhgd,bkhd->bhgqk', qi, k, preferred_element_type=jnp.float32) * scale
        p = jax.nn.softmax(s, axis=-1).astype(v.dtype)
        return jnp.einsum('bhgqk,bkhd->bqhgd', p, v)

    o = from_blocks(lax.map(block, qb))
    return o.reshape(b, lq, A_HEADS * HEAD_DIM)


def diff_attention(q, k, v, lam):
    qb = to_blocks(q)
    scale = HEAD_DIM ** -0.5

    def block(qi):
        s = jnp.einsum('bqhcd,bkhcd->bhcqk', qi, k, preferred_element_type=jnp.float32) * scale
        p = jax.nn.softmax(s, axis=-1)
        a = (p[:, :, 0] - lam * p[:, :, 1]).astype(v.dtype)
        return jnp.einsum('bhqk,bkhe->bqhe', a, v)

    return from_blocks(lax.map(block, qb))


def conformer_conv(u, dw_w, dw_b, ln_g, ln_b):
    a, gt = jnp.split(u, 2, axis=-1)
    h = a * jax.nn.sigmoid(gt)
    h = lax.conv_general_dilated(h, dw_w[:, None, :].astype(h.dtype), window_strides=(1,),
                                 padding=[(CONV_W // 2, CONV_W // 2)],
                                 dimension_numbers=('NWC', 'WIO', 'NWC'),
                                 feature_group_count=C_CONV) + dw_b
    return jax.nn.silu(layer_norm(h, ln_g, ln_b))


def even_mixer(h_ctx, h_lat, cos, sin, w_in, w_out, q_g, k_g, dw_w, dw_b, ln_g, ln_b, need_ctx_out):
    splits = [A_HEADS * HEAD_DIM, (A_HEADS + A_KV_HEADS) * HEAD_DIM, (A_HEADS + 2 * A_KV_HEADS) * HEAD_DIM]

    def project(h, rope):
        b, l = h.shape[:2]
        q, k, v, u = jnp.split(h @ w_in, splits, axis=-1)
        q = rms_norm(q.reshape(b, l, A_HEADS, HEAD_DIM), q_g)
        k = rms_norm(k.reshape(b, l, A_KV_HEADS, HEAD_DIM), k_g)
        v = v.reshape(b, l, A_KV_HEADS, HEAD_DIM)
        if rope:
            q, k = apply_rope(q, cos, sin), apply_rope(k, cos, sin)
        return q, k, v, u

    qc, kc, vc, uc = project(h_ctx, False)
    ql, kl, vl, ul = project(h_lat, True)
    k_all = jnp.concatenate([kc, kl], axis=1)
    v_all = jnp.concatenate([vc, vl], axis=1)
    out_lat = jnp.concatenate([gqa_attention(ql, k_all, v_all),
                               conformer_conv(ul, dw_w, dw_b, ln_g, ln_b)], axis=-1) @ w_out
    out_ctx = None
    if need_ctx_out:
        out_ctx = jnp.concatenate([gqa_attention(qc, kc, vc),
                                   conformer_conv(uc, dw_w, dw_b, ln_g, ln_b)], axis=-1) @ w_out
    return out_ctx, out_lat


def odd_mixer(h_ctx, h_lat, cos, sin, w_in, w_out, lam_p, subln_g, lam_init, need_ctx_out):
    lp = lam_p.astype(jnp.float32)
    lam = jnp.exp(jnp.sum(lp[0] * lp[1])) - jnp.exp(jnp.sum(lp[2] * lp[3])) + lam_init

    def project(h, rope):
        b, l = h.shape[:2]
        q, k, v = jnp.split(h @ w_in, 3, axis=-1)
        q = q.reshape(b, l, 2 * DIFF_HEADS, HEAD_DIM)
        k = k.reshape(b, l, 2 * DIFF_HEADS, HEAD_DIM)
        if rope:
            q, k = apply_rope(q, cos, sin), apply_rope(k, cos, sin)
        return (q.reshape(b, l, DIFF_HEADS, 2, HEAD_DIM), k.reshape(b, l, DIFF_HEADS, 2, HEAD_DIM),
                v.reshape(b, l, DIFF_HEADS, 2 * HEAD_DIM))

    def finish(o):
        b, l = o.shape[:2]
        o = rms_norm(o, subln_g) * (1 - lam_init)
        return o.reshape(b, l, D_MIX) @ w_out

    qc, kc, vc = project(h_ctx, False)
    ql, kl, vl = project(h_lat, True)
    k_all = jnp.concatenate([kc, kl], axis=1)
    v_all = jnp.concatenate([vc, vl], axis=1)
    out_lat = finish(diff_attention(ql, k_all, v_all, lam))
    out_ctx = finish(diff_attention(qc, kc, vc, lam)) if need_ctx_out else None
    return out_ctx, out_lat


def swiglu(h, wg, wu, wd):
    return (jax.nn.silu(h @ wg) * (h @ wu)) @ wd


def moe_ffn(h, router, bias, wg, wu, wd, sg, su, sd):
    n, d = h.shape
    scores = jax.nn.sigmoid(jnp.einsum('nd,de->ne', h, router, preferred_element_type=jnp.float32))
    choice = scores + bias.astype(jnp.float32)
    grp = lax.top_k(choice.reshape(n, N_GROUPS, N_EXPERTS // N_GROUPS), 2)[0].sum(-1)
    _, top_g = lax.top_k(grp, TOPK_GROUPS)
    gmask = jnp.any(top_g[..., None] == jnp.arange(N_GROUPS), axis=1)
    emask = jnp.repeat(gmask, N_EXPERTS // N_GROUPS, axis=1)
    _, top_e = lax.top_k(jnp.where(emask, choice, -jnp.inf), TOP_K)
    w = jnp.take_along_axis(scores, top_e, axis=1)
    w = w / jnp.sum(w, axis=-1, keepdims=True) * ROUTED_SCALE
    a = n * TOP_K
    e_flat = top_e.reshape(a)
    tok_flat = jnp.repeat(jnp.arange(n, dtype=jnp.int32), TOP_K)
    w_flat = w.reshape(a)
    counts = jnp.zeros((N_EXPERTS,), jnp.int32).at[e_flat].add(1)
    padded = (counts + MOE_BLOCK - 1) // MOE_BLOCK * MOE_BLOCK
    pad_end = jnp.cumsum(padded)
    pad_start = pad_end - padded
    start = jnp.cumsum(counts) - counts
    order = jnp.argsort(e_flat)
    se = e_flat[order]
    dest = pad_start[se] + jnp.arange(a, dtype=jnp.int32) - start[se]
    cap = a + N_EXPERTS * MOE_BLOCK
    nb = cap // MOE_BLOCK
    slot_tok = jnp.full((cap,), n, jnp.int32).at[dest].set(tok_flat[order])
    slot_w = jnp.zeros((cap,), h.dtype).at[dest].set(w_flat[order].astype(h.dtype))
    block_e = jnp.minimum(jnp.searchsorted(pad_end, jnp.arange(nb, dtype=jnp.int32) * MOE_BLOCK, side='right'),
                          N_EXPERTS - 1)
    h_pad = jnp.concatenate([h, jnp.zeros((1, d), h.dtype)], axis=0)

    def expert_block(args):
        toks, e = args
        return swiglu(h_pad[toks], wg[e], wu[e], wd[e])

    y = lax.map(expert_block, (slot_tok.reshape(nb, MOE_BLOCK), block_e)).reshape(cap, d)
    routed = jax.ops.segment_sum(y * slot_w[:, None], slot_tok, num_segments=n + 1)[:n]
    return routed + swiglu(h, sg, su, sd)


def setup_inputs(seed: int = 0) -> dict:
    key = jax.random.key(seed)
    ks = iter(jax.random.split(key, 40))

    def nrm(shape, scale):
        return jax.random.normal(next(ks), shape, jnp.float32) * scale

    def gain(shape):
        return 1.0 + nrm(shape, 0.02)

    d = D_MODEL
    return {
        'x': nrm((BATCH, SEQ, d), 1.0),
        'c': nrm((BATCH, d), 1.0),
        'ctx': nrm((BATCH, CTX_LEN, d), 1.0),
        'c_ctx': nrm((d,), 1.0),
        'ada_w': nrm((DEPTH, d, 6 * d), 0.5 * d ** -0.5),
        'ada_b': nrm((DEPTH, 6 * d), 0.02),
        'norm_g': gain((DEPTH, 4, d)),
        'ev_w_in': nrm((N_EVEN, d, IN_EVEN), d ** -0.5),
        'ev_w_out': nrm((N_EVEN, D_MIX, d), D_MIX ** -0.5),
        'ev_q_norm': gain((N_EVEN, HEAD_DIM)),
        'ev_k_norm': gain((N_EVEN, HEAD_DIM)),
        'ev_dw_w': nrm((N_EVEN, CONV_W, C_CONV), CONV_W ** -0.5),
        'ev_dw_b': nrm((N_EVEN, C_CONV), 0.02),
        'ev_ln_g': gain((N_EVEN, C_CONV)),
        'ev_ln_b': nrm((N_EVEN, C_CONV), 0.02),
        'od_w_in': nrm((N_ODD, d, IN_ODD), d ** -0.5),
        'od_w_out': nrm((N_ODD, D_MIX, d), D_MIX ** -0.5),
        'od_lambda': nrm((N_ODD, 4, HEAD_DIM), LAMBDA_STD),
        'od_subln_g': gain((N_ODD, 2 * HEAD_DIM)),
        'moe_router': nrm((DEPTH, d, N_EXPERTS), d ** -0.5),
        'moe_bias': nrm((DEPTH, N_EXPERTS), 0.01),
        'moe_w_gate': nrm((DEPTH, N_EXPERTS, d, EXPERT_FF), d ** -0.5),
        'moe_w_up': nrm((DEPTH, N_EXPERTS, d, EXPERT_FF), d ** -0.5),
        'moe_w_down': nrm((DEPTH, N_EXPERTS, EXPERT_FF, d), EXPERT_FF ** -0.5),
        'sh_w_gate': nrm((DEPTH, d, EXPERT_FF), d ** -0.5),
        'sh_w_up': nrm((DEPTH, d, EXPERT_FF), d ** -0.5),
        'sh_w_down': nrm((DEPTH, EXPERT_FF, d), EXPERT_FF ** -0.5),
    }


def reference(x, c, ctx, c_ctx, ada_w, ada_b, norm_g, ev_w_in, ev_w_out, ev_q_norm, ev_k_norm, ev_dw_w, ev_dw_b,
              ev_ln_g, ev_ln_b, od_w_in, od_w_out, od_lambda, od_subln_g, moe_router, moe_bias, moe_w_gate,
              moe_w_up, moe_w_down, sh_w_gate, sh_w_up, sh_w_down):
    b, seq, d = x.shape
    ctx_len = ctx.shape[1]
    cos, sin = axial_rope_tables(seq, x.dtype)
    silu_c = jax.nn.silu(c)
    silu_cc = jax.nn.silu(c_ctx)[None, :]
    h_lat, h_ctx = x, ctx
    for layer in range(DEPTH):
        last = layer == DEPTH - 1
        i = layer // 2
        mod_l = jnp.split((silu_c @ ada_w[layer] + ada_b[layer])[:, None, :], 6, axis=-1)
        mod_c = jnp.split((silu_cc @ ada_w[layer] + ada_b[layer])[:, None, :], 6, axis=-1)
        a_lat = modulate(h_lat, norm_g[layer, 0], mod_l[0], mod_l[1])
        a_ctx = modulate(h_ctx, norm_g[layer, 0], mod_c[0], mod_c[1])
        if layer % 2 == 0:
            out_ctx, out_lat = even_mixer(a_ctx, a_lat, cos, sin, ev_w_in[i], ev_w_out[i], ev_q_norm[i],
                                          ev_k_norm[i], ev_dw_w[i], ev_dw_b[i], ev_ln_g[i], ev_ln_b[i],
                                          need_ctx_out=not last)
        else:
            out_ctx, out_lat = odd_mixer(a_ctx, a_lat, cos, sin, od_w_in[i], od_w_out[i], od_lambda[i],
                                         od_subln_g[i], 0.8 - 0.6 * math.exp(-0.3 * layer),
                                         need_ctx_out=not last)
        h_lat = h_lat + mod_l[2] * rms_norm(out_lat, norm_g[layer, 1])
        f_lat = modulate(h_lat, norm_g[layer, 2], mod_l[3], mod_l[4])
        moe_args = (moe_router[layer], moe_bias[layer], moe_w_gate[layer], moe_w_up[layer], moe_w_down[layer],
                    sh_w_gate[layer], sh_w_up[layer], sh_w_down[layer])
        if last:
            y_lat = moe_ffn(f_lat.reshape(-1, d), *moe_args).reshape(b, seq, d)
        else:
            h_ctx = h_ctx + mod_c[2] * rms_norm(out_ctx, norm_g[layer, 1])
            f_ctx = modulate(h_ctx, norm_g[layer, 2], mod_c[3], mod_c[4])
            n_ctx = b * ctx_len
            y = moe_ffn(jnp.concatenate([f_ctx.reshape(-1, d), f_lat.reshape(-1, d)], axis=0), *moe_args)
            h_ctx = h_ctx + mod_c[5] * rms_norm(y[:n_ctx].reshape(b, ctx_len, d), norm_g[layer, 3])
            y_lat = y[n_ctx:].reshape(b, seq, d)
        h_lat = h_lat + mod_l[5] * rms_norm(y_lat, norm_g[layer, 3])
    return h_lat
```

```python
import functools
import math

import jax
import jax.numpy as jnp
from jax import lax
from jax.experimental import pallas as pl
from jax.experimental.pallas import tpu as pltpu

F32 = jnp.float32
BF16 = jnp.bfloat16
U32 = jnp.uint32
I32 = jnp.int32

GRID_W = 64
HEAD_DIM = 128
ROPE_THETA = 10000.0
EPS = 1e-6
A_HEADS = 8
A_KV_HEADS = 2
CONV_W = 31
DIFF_HEADS = 8
N_EXPERTS = 64
TOP_K = 8
N_GROUPS = 8
TOPK_GROUPS = 4
ROUTED_SCALE = 2.5

LANES = 128
SUBLANES = 8
VMEM_LIMIT_BYTES = 56 * 1024 * 1024

ROW_TILE = 256
KV_CHUNK = 256
Q_TILE_GQA = 128
Q_TILE_DIFF = 256
ROUTE_TILE = 256
EXPERT_TILE = 256
CONV_HALO = 16
NEG_BIG = -1e30


def _params(*sem):
    return pltpu.CompilerParams(dimension_semantics=sem, vmem_limit_bytes=VMEM_LIMIT_BYTES)


def _sigmoid(x):
    return 1.0 / (1.0 + jnp.exp(-x))


def _rms(x, g):
    return x * lax.rsqrt(jnp.mean(x * x, axis=-1, keepdims=True) + EPS) * g


def _modulate(x, g, shift, scale):
    return _rms(x, g) * (1.0 + scale) + shift


def _pack_halves(x):
    c = x.shape[1] // 2
    lo = lax.bitcast_convert_type(x[:, :c].astype(BF16).astype(F32), U32)
    hi = lax.bitcast_convert_type(x[:, c:].astype(BF16).astype(F32), U32)
    return (hi & jnp.uint32(0xFFFF0000)) | (lo >> 16)


def _unpack_halves(w):
    lo = lax.bitcast_convert_type(w << 16, F32)
    hi = lax.bitcast_convert_type(w & jnp.uint32(0xFFFF0000), F32)
    return lo, hi


ADA_COLS = 512
ADA_KCHUNK = 64


def _ada_kernel(cv_ref, w_ref, b_ref, o_ref, s_ref):
    cv = cv_ref[...]
    s_ref[...] = cv * _sigmoid(cv)
    d = cv.shape[0]
    tn = o_ref.shape[-1]

    def body(k, acc):
        a0, a1 = acc
        r0 = pl.multiple_of(k * ADA_KCHUNK, ADA_KCHUNK)
        w = w_ref[0, pl.ds(r0, ADA_KCHUNK), :]
        s = s_ref[pl.ds(r0, ADA_KCHUNK), :]
        p0 = (w * s[:, 0:1]).reshape(ADA_KCHUNK // SUBLANES, SUBLANES, tn)
        p1 = (w * s[:, 1:2]).reshape(ADA_KCHUNK // SUBLANES, SUBLANES, tn)
        return a0 + jnp.sum(p0, axis=0), a1 + jnp.sum(p1, axis=0)

    z = jnp.zeros((SUBLANES, tn), F32)
    a0, a1 = lax.fori_loop(0, d // ADA_KCHUNK, body, (z, z))
    b = b_ref[0]
    o_ref[0, 0:1, :] = jnp.sum(a0, axis=0, keepdims=True) + b
    o_ref[0, 1:2, :] = jnp.sum(a1, axis=0, keepdims=True) + b


def _ada(cvec, ada_w, ada_b):
    depth, d, n = ada_w.shape
    return pl.pallas_call(
        _ada_kernel,
        grid=(depth, n // ADA_COLS),
        in_specs=[
            pl.BlockSpec((d, 2), lambda l, j: (0, 0)),
            pl.BlockSpec((1, d, ADA_COLS), lambda l, j: (l, 0, j)),
            pl.BlockSpec((1, 1, ADA_COLS), lambda l, j: (l, 0, j)),
        ],
        out_specs=pl.BlockSpec((1, 2, ADA_COLS), lambda l, j: (l, 0, j)),
        out_shape=jax.ShapeDtypeStruct((depth, 2, n), F32),
        scratch_shapes=[pltpu.VMEM((d, 2), F32)],
        compiler_params=_params("arbitrary", "arbitrary"),
        name="ada",
    )(cvec, ada_w, ada_b.reshape(depth, 1, n))


def _modmm_kernel(h_ref, mod_ref, g_ref, w_ref, o_ref):
    a = _modulate(h_ref[...], g_ref[...], mod_ref[0, 0:1, :], mod_ref[0, 1:2, :])
    o_ref[...] = jnp.dot(a.astype(BF16), w_ref[...], preferred_element_type=F32).astype(o_ref.dtype)


def _modmm(h, mod, g, w, n_ctx, col_tile):
    r, d = h.shape
    n = w.shape[1]
    ctx_tiles = n_ctx // ROW_TILE
    return pl.pallas_call(
        _modmm_kernel,
        grid=(n // col_tile, r // ROW_TILE),
        in_specs=[
            pl.BlockSpec((ROW_TILE, d), lambda j, i: (i, 0)),
            pl.BlockSpec((1, 6, d), lambda j, i: (jnp.where(i < ctx_tiles, 1, 0), 0, 0)),
            pl.BlockSpec((1, d), lambda j, i: (0, 0)),
            pl.BlockSpec((d, col_tile), lambda j, i: (0, j)),
        ],
        out_specs=pl.BlockSpec((ROW_TILE, col_tile), lambda j, i: (i, j)),
        out_shape=jax.ShapeDtypeStruct((r, n), BF16),
        compiler_params=_params("arbitrary", "arbitrary"),
        name="modmm",
    )(h, mod, g, w)


def _rope(x, c, s):
    lane = lax.broadcasted_iota(I32, x.shape, 1)
    first = (lane % 64) < 32
    partner = jnp.where(first, pltpu.roll(x, 96, axis=1), pltpu.roll(x, 32, axis=1))
    return x * c + partner * s


def _prep_kernel(x_ref, c_ref, s_ref, g_ref, o_ref, *, norm, scale, transpose):
    x = x_ref[...].astype(F32)
    if norm:
        x = _rms(x, g_ref[...])
    y = _rope(x, c_ref[...], s_ref[...]) * scale
    if transpose:
        o_ref[0, 0] = y.T.astype(o_ref.dtype)
    else:
        o_ref[...] = y.astype(o_ref.dtype)


def _prep_q(x, col0, n_heads, cos, sin, g, norm, scale):
    r = x.shape[0]
    cb = col0 // HEAD_DIM
    return pl.pallas_call(
        functools.partial(_prep_kernel, norm=norm, scale=scale, transpose=False),
        grid=(r // ROW_TILE, n_heads),
        in_specs=[
            pl.BlockSpec((ROW_TILE, HEAD_DIM), lambda i, h: (i, cb + h)),
            pl.BlockSpec((ROW_TILE, HEAD_DIM), lambda i, h: (i, 0)),
            pl.BlockSpec((ROW_TILE, HEAD_DIM), lambda i, h: (i, 0)),
            pl.BlockSpec((1, HEAD_DIM), lambda i, h: (0, 0)),
        ],
        out_specs=pl.BlockSpec((ROW_TILE, HEAD_DIM), lambda i, h: (i, h)),
        out_shape=jax.ShapeDtypeStruct((r, n_heads * HEAD_DIM), BF16),
        compiler_params=_params("arbitrary", "arbitrary"),
        name="prep_q",
    )(x, cos, sin, g)


def _prep_k(x, col0, n_heads, cos, sin, g, norm):
    r = x.shape[0]
    cb = col0 // HEAD_DIM
    nc = r // KV_CHUNK
    return pl.pallas_call(
        functools.partial(_prep_kernel, norm=norm, scale=1.0, transpose=True),
        grid=(nc, n_heads),
        in_specs=[
            pl.BlockSpec((KV_CHUNK, HEAD_DIM), lambda i, h: (i, cb + h)),
            pl.BlockSpec((KV_CHUNK, HEAD_DIM), lambda i, h: (i, 0)),
            pl.BlockSpec((KV_CHUNK, HEAD_DIM), lambda i, h: (i, 0)),
            pl.BlockSpec((1, HEAD_DIM), lambda i, h: (0, 0)),
        ],
        out_specs=pl.BlockSpec((1, 1, HEAD_DIM, KV_CHUNK), lambda i, h: (h, i, 0, 0)),
        out_shape=jax.ShapeDtypeStruct((n_heads, nc, HEAD_DIM, KV_CHUNK), BF16),
        compiler_params=_params("arbitrary", "arbitrary"),
        name="prep_k",
    )(x, cos, sin, g)


def _softmax_step(s, v, m_ref, l_ref, acc_ref):
    m_prev = m_ref[...]
    m_new = jnp.maximum(m_prev, jnp.max(s, axis=-1, keepdims=True))
    alpha = jnp.exp2(m_prev - m_new)
    p = jnp.exp2(s - m_new[:, 0:1])
    l_ref[...] = alpha * l_ref[...] + jnp.sum(p, axis=-1, keepdims=True)
    acc_ref[...] = alpha[:, 0:1] * acc_ref[...] + jnp.dot(p.astype(BF16), v, preferred_element_type=F32)
    m_ref[...] = m_new


def _gqa_kernel(q_ref, kt_ref, v_ref, o_ref, qs_ref, m_ref, l_ref, acc_ref, *, ctx_tiles, ctx_chunks, all_chunks):
    tq = q_ref.shape[0]
    group = q_ref.shape[1] // HEAD_DIM
    for g in range(group):
        qs_ref[g * tq:(g + 1) * tq, :] = q_ref[:, g * HEAD_DIM:(g + 1) * HEAD_DIM]
    m_ref[...] = jnp.full(m_ref.shape, NEG_BIG, F32)
    l_ref[...] = jnp.zeros(l_ref.shape, F32)
    acc_ref[...] = jnp.zeros(acc_ref.shape, F32)
    n_chunks = jnp.where(pl.program_id(1) < ctx_tiles, ctx_chunks, all_chunks)

    def body(c, carry):
        s = jnp.dot(qs_ref[...], kt_ref[0, c], preferred_element_type=F32)
        v = v_ref[pl.ds(pl.multiple_of(c * KV_CHUNK, KV_CHUNK), KV_CHUNK), :]
        _softmax_step(s, v, m_ref, l_ref, acc_ref)
        return carry

    lax.fori_loop(0, n_chunks, body, 0)
    o = acc_ref[...] / l_ref[:, 0:1]
    for g in range(group):
        o_ref[:, g * HEAD_DIM:(g + 1) * HEAD_DIM] = o[g * tq:(g + 1) * tq, :].astype(o_ref.dtype)


def _gqa_attention(q, kt, qkvu, v_col0, n_ctx):
    r = q.shape[0]
    group = A_HEADS // A_KV_HEADS
    tq = Q_TILE_GQA
    nc = r // KV_CHUNK
    vb = v_col0 // HEAD_DIM
    kern = functools.partial(_gqa_kernel, ctx_tiles=n_ctx // tq, ctx_chunks=n_ctx // KV_CHUNK, all_chunks=nc)
    return pl.pallas_call(
        kern,
        grid=(A_KV_HEADS, r // tq),
        in_specs=[
            pl.BlockSpec((tq, group * HEAD_DIM), lambda g, i: (i, g)),
            pl.BlockSpec((1, nc, HEAD_DIM, KV_CHUNK), lambda g, i: (g, 0, 0, 0)),
            pl.BlockSpec((r, HEAD_DIM), lambda g, i: (0, vb + g)),
        ],
        out_specs=pl.BlockSpec((tq, group * HEAD_DIM), lambda g, i: (i, g)),
        out_shape=jax.ShapeDtypeStruct((r, A_HEADS * HEAD_DIM), BF16),
        scratch_shapes=[
            pltpu.VMEM((group * tq, HEAD_DIM), BF16),
            pltpu.VMEM((group * tq, LANES), F32),
            pltpu.VMEM((group * tq, LANES), F32),
            pltpu.VMEM((group * tq, HEAD_DIM), F32),
        ],
        compiler_params=_params("arbitrary", "arbitrary"),
        name="gqa_attn",
    )(q, kt, qkvu)


def _diff_kernel(lam_ref, q_ref, kt_ref, v_ref, g_ref, o_ref, m0, l0, a0, m1, l1, a1, *, out_scale, n_chunks):
    for m_ref, l_ref, a_ref in ((m0, l0, a0), (m1, l1, a1)):
        m_ref[...] = jnp.full(m_ref.shape, NEG_BIG, F32)
        l_ref[...] = jnp.zeros(l_ref.shape, F32)
        a_ref[...] = jnp.zeros(a_ref.shape, F32)

    def body(c, carry):
        v = v_ref[pl.ds(pl.multiple_of(c * KV_CHUNK, KV_CHUNK), KV_CHUNK), :]
        s0 = jnp.dot(q_ref[:, 0:HEAD_DIM], kt_ref[0, c], preferred_element_type=F32)
        _softmax_step(s0, v, m0, l0, a0)
        s1 = jnp.dot(q_ref[:, HEAD_DIM:2 * HEAD_DIM], kt_ref[1, c], preferred_element_type=F32)
        _softmax_step(s1, v, m1, l1, a1)
        return carry

    lax.fori_loop(0, n_chunks, body, 0)
    o = a0[...] / l0[:, 0:1] - lam_ref[0] * (a1[...] / l1[:, 0:1])
    o_ref[...] = (_rms(o, g_ref[...]) * out_scale).astype(o_ref.dtype)


def _diff_attention(lam, q, kt, qkv, v_col0, subln_g, out_scale, n_ctx):
    r = q.shape[0]
    seq = r - n_ctx
    tq = Q_TILE_DIFF
    nc = r // KV_CHUNK
    dv = 2 * HEAD_DIM
    vb = v_col0 // dv
    off = n_ctx // tq
    kern = functools.partial(_diff_kernel, out_scale=out_scale, n_chunks=nc)
    stat = pltpu.VMEM((tq, LANES), F32)
    acc = pltpu.VMEM((tq, dv), F32)
    return pl.pallas_call(
        kern,
        grid=(DIFF_HEADS, seq // tq),
        in_specs=[
            pl.BlockSpec(memory_space=pltpu.SMEM),
            pl.BlockSpec((tq, dv), lambda h, i: (i + off, h)),
            pl.BlockSpec((2, nc, HEAD_DIM, KV_CHUNK), lambda h, i: (h, 0, 0, 0)),
            pl.BlockSpec((r, dv), lambda h, i: (0, vb + h)),
            pl.BlockSpec((1, dv), lambda h, i: (0, 0)),
        ],
        out_specs=pl.BlockSpec((tq, dv), lambda h, i: (i, h)),
        out_shape=jax.ShapeDtypeStruct((seq, DIFF_HEADS * dv), BF16),
        scratch_shapes=[stat, stat, acc, stat, stat, acc],
        compiler_params=_params("arbitrary", "arbitrary"),
        name="diff_attn",
    )(lam, q, kt, qkv, subln_g)


def _conv_kernel(ap_ref, gp_ref, ac_ref, gc_ref, an_ref, gn_ref, w_ref, b_ref, lg_ref, lb_ref, o_ref, h_ref,
                 *, seq_starts, seq_ends):
    i = pl.program_id(0)
    tr = ac_ref.shape[0]
    is_start = functools.reduce(jnp.logical_or, [i == s for s in seq_starts])
    is_end = functools.reduce(jnp.logical_or, [i == e for e in seq_ends])

    def glu(a_ref, g_ref):
        return a_ref[...].astype(F32) * _sigmoid(g_ref[...].astype(F32))

    h_ref[0:CONV_HALO, :] = jnp.where(is_start, 0.0, glu(ap_ref, gp_ref))
    h_ref[CONV_HALO:CONV_HALO + tr, :] = glu(ac_ref, gc_ref)
    h_ref[CONV_HALO + tr:, :] = jnp.where(is_end, 0.0, glu(an_ref, gn_ref))
    acc = jnp.zeros((tr, ac_ref.shape[1]), F32)
    base = CONV_HALO - CONV_W // 2
    for j in range(CONV_W):
        acc = acc + w_ref[j:j + 1, :] * h_ref[base + j:base + j + tr, :]
    y = acc + b_ref[...]
    mu = jnp.mean(y, axis=-1, keepdims=True)
    var = jnp.mean(jnp.square(y - mu), axis=-1, keepdims=True)
    z = (y - mu) * lax.rsqrt(var + EPS) * lg_ref[...] + lb_ref[...]
    o_ref[...] = (z * _sigmoid(z)).astype(o_ref.dtype)


CONV_TILE = 128


def _conformer_conv(qkvu, u_col0, c_conv, dw_w, dw_b, ln_g, ln_b, n_ctx):
    r = qkvu.shape[0]
    tr = CONV_TILE
    nt = r // tr
    hb = tr // CONV_HALO
    ab = u_col0 // c_conv
    gb = ab + 1
    last_halo = r // CONV_HALO - 1
    seq_starts = (0, n_ctx // tr)
    seq_ends = (n_ctx // tr - 1, nt - 1)

    def prev(col):
        return pl.BlockSpec((CONV_HALO, c_conv), lambda i: (jnp.maximum(i * hb - 1, 0), col))

    def cur(col):
        return pl.BlockSpec((tr, c_conv), lambda i: (i, col))

    def nxt(col):
        return pl.BlockSpec((CONV_HALO, c_conv), lambda i: (jnp.minimum((i + 1) * hb, last_halo), col))

    vec = pl.BlockSpec((1, c_conv), lambda i: (0, 0))
    return pl.pallas_call(
        functools.partial(_conv_kernel, seq_starts=seq_starts, seq_ends=seq_ends),
        grid=(nt,),
        in_specs=[prev(ab), prev(gb), cur(ab), cur(gb), nxt(ab), nxt(gb),
                  pl.BlockSpec((CONV_W, c_conv), lambda i: (0, 0)), vec, vec, vec],
        out_specs=pl.BlockSpec((tr, c_conv), lambda i: (i, 0)),
        out_shape=jax.ShapeDtypeStruct((r, c_conv), BF16),
        scratch_shapes=[pltpu.VMEM((tr + 2 * CONV_HALO, c_conv), F32)],
        compiler_params=_params("arbitrary"),
        name="conformer_conv",
    )(qkvu, qkvu, qkvu, qkvu, qkvu, qkvu, dw_w, dw_b, ln_g, ln_b)


def _post_kernel(*refs, n_in):
    o_refs = refs[:n_in]
    w_ref, h_ref, mod_ref, g_ref, rr_ref, h1_ref, fp_ref, lg_ref = refs[n_in:]
    k0 = 0
    out = None
    for o_in in o_refs:
        kw = o_in.shape[1]
        part = jnp.dot(o_in[...], w_ref[k0:k0 + kw, :], preferred_element_type=F32)
        out = part if out is None else out + part
        k0 += kw
    h1 = h_ref[...] + mod_ref[0, 2:3, :] * _rms(out, g_ref[0:1, :])
    h1_ref[...] = h1
    f = _modulate(h1, g_ref[1:2, :], mod_ref[0, 3:4, :], mod_ref[0, 4:5, :])
    fp_ref[...] = _pack_halves(f)
    f_hi = f.astype(BF16)
    f_lo = (f - f_hi.astype(F32)).astype(BF16)
    l2 = (jnp.dot(f_hi, rr_ref[...], preferred_element_type=F32)
          + jnp.dot(f_lo, rr_ref[...], preferred_element_type=F32))
    lg_ref[...] = l2 + pltpu.roll(l2, N_EXPERTS, axis=1)


def _post(o_parts, w_out, h, row_off, mod, g12, rr, n_ctx):
    n = o_parts[0].shape[0]
    d = h.shape[1]
    tm = ROW_TILE
    off = row_off // tm
    ctx_tiles = (n_ctx - row_off) // tm
    in_specs = [pl.BlockSpec((tm, o.shape[1]), lambda i: (i, 0)) for o in o_parts]
    in_specs += [
        pl.BlockSpec((d, d), lambda i: (0, 0)),
        pl.BlockSpec((tm, d), lambda i: (i + off, 0)),
        pl.BlockSpec((1, 6, d), lambda i: (jnp.where(i < ctx_tiles, 1, 0), 0, 0)),
        pl.BlockSpec((2, d), lambda i: (0, 0)),
        pl.BlockSpec((d, 2 * N_EXPERTS), lambda i: (0, 0)),
    ]
    return pl.pallas_call(
        functools.partial(_post_kernel, n_in=len(o_parts)),
        grid=(n // tm,),
        in_specs=in_specs,
        out_specs=[
            pl.BlockSpec((tm, d), lambda i: (i, 0)),
            pl.BlockSpec((tm, d // 2), lambda i: (i, 0)),
            pl.BlockSpec((tm, 2 * N_EXPERTS), lambda i: (i, 0)),
        ],
        out_shape=[
            jax.ShapeDtypeStruct((n, d), F32),
            jax.ShapeDtypeStruct((n, d // 2), U32),
            jax.ShapeDtypeStruct((n, 2 * N_EXPERTS), F32),
        ],
        compiler_params=_params("arbitrary"),
        name="post_mixer",
    )(*o_parts, w_out, h, mod, g12, rr)


def _route_kernel(lg_ref, bias_ref, e_ref, w_ref, rank_ref, cnt_ref):
    t = lg_ref.shape[0]
    gsz = N_EXPERTS // N_GROUPS

    @pl.when(pl.program_id(0) == 0)
    def _():
        cnt_ref[...] = jnp.zeros(cnt_ref.shape, F32)

    logits = lg_ref[...].T[0:N_EXPERTS, :]
    scores = _sigmoid(logits)
    choice = scores + bias_ref[...]
    sub = lax.broadcasted_iota(I32, (gsz, t), 0).astype(F32)
    grp_rows = []
    for g in range(N_GROUPS):
        cg = choice[g * gsz:(g + 1) * gsz, :]
        m1 = jnp.max(cg, axis=0, keepdims=True)
        i1 = jnp.min(jnp.where(cg == m1, sub, float(gsz)), axis=0, keepdims=True)
        m2 = jnp.max(jnp.where(sub == i1, -jnp.inf, cg), axis=0, keepdims=True)
        grp_rows.append(m1 + m2)
    grp = jnp.concatenate(grp_rows, axis=0)
    gidx = lax.broadcasted_iota(I32, (N_GROUPS, t), 0).astype(F32)
    gsel = jnp.zeros((N_GROUPS, t), F32)
    for _ in range(TOPK_GROUPS):
        gm = jnp.max(grp, axis=0, keepdims=True)
        gi = jnp.min(jnp.where(grp == gm, gidx, float(N_GROUPS)), axis=0, keepdims=True)
        hit = gidx == gi
        gsel = jnp.where(hit, 1.0, gsel)
        grp = jnp.where(hit, -jnp.inf, grp)
    eidx = lax.broadcasted_iota(I32, (N_EXPERTS, t), 0).astype(F32)
    mc_rows = []
    for g in range(N_GROUPS):
        mc_rows.append(jnp.where(gsel[g:g + 1, :] > 0.5, choice[g * gsz:(g + 1) * gsz, :], -jnp.inf))
    mc = jnp.concatenate(mc_rows, axis=0)
    sel_all = jnp.zeros((N_EXPERTS, t), F32)
    e_rows, w_rows = [], []
    for _ in range(TOP_K):
        mx = jnp.max(mc, axis=0, keepdims=True)
        ei = jnp.min(jnp.where(mc == mx, eidx, float(N_EXPERTS)), axis=0, keepdims=True)
        hit = eidx == ei
        e_rows.append(ei)
        w_rows.append(jnp.sum(jnp.where(hit, scores, 0.0), axis=0, keepdims=True))
        sel_all = jnp.where(hit, 1.0, sel_all)
        mc = jnp.where(hit, -jnp.inf, mc)
    e_all = jnp.concatenate(e_rows, axis=0)
    w_all = jnp.concatenate(w_rows, axis=0)
    w_ref[...] = w_all / jnp.sum(w_all, axis=0, keepdims=True) * ROUTED_SCALE
    e_ref[...] = e_all.astype(I32)
    src = lax.broadcasted_iota(I32, (t, t), 0)
    dst = lax.broadcasted_iota(I32, (t, t), 1)
    upper = jnp.where(src < dst, 1.0, 0.0).astype(BF16)
    pos = jnp.dot(sel_all.astype(BF16), upper, preferred_element_type=F32) + cnt_ref[:, 0:1]
    r_rows = [jnp.sum(jnp.where(eidx == e_rows[k], pos, 0.0), axis=0, keepdims=True) for k in range(TOP_K)]
    rank_ref[...] = jnp.concatenate(r_rows, axis=0).astype(I32)
    cnt_ref[...] = cnt_ref[...] + jnp.sum(sel_all, axis=1, keepdims=True)


def _route(logits, bias):
    n = logits.shape[0]
    t = ROUTE_TILE
    row = pl.BlockSpec((TOP_K, t), lambda i: (0, i))
    return pl.pallas_call(
        _route_kernel,
        grid=(n // t,),
        in_specs=[
            pl.BlockSpec((t, 2 * N_EXPERTS), lambda i: (i, 0)),
            pl.BlockSpec((N_EXPERTS, 1), lambda i: (0, 0)),
        ],
        out_specs=[row, row, row, pl.BlockSpec((N_EXPERTS, LANES), lambda i: (0, 0))],
        out_shape=[
            jax.ShapeDtypeStruct((TOP_K, n), I32),
            jax.ShapeDtypeStruct((TOP_K, n), F32),
            jax.ShapeDtypeStruct((TOP_K, n), I32),
            jax.ShapeDtypeStruct((N_EXPERTS, LANES), F32),
        ],
        compiler_params=_params("arbitrary"),
        name="route",
    )(logits, bias)


def _dispatch_kernel(slot_hbm, fp_ref, xs_in, xs_hbm, slot_smem, sem_idx, sem_rows):
    del xs_in
    i = pl.program_id(0)
    t = fp_ref.shape[0]
    idx_copy = pltpu.make_async_copy(slot_hbm.at[i], slot_smem, sem_idx)
    idx_copy.start()
    idx_copy.wait()

    def issue(tok, carry):
        for k in range(TOP_K):
            s = slot_smem[k * t + tok]
            pltpu.make_async_copy(fp_ref.at[pl.ds(tok, 1), :], xs_hbm.at[pl.ds(s, 1), :], sem_rows).start()
        return carry

    lax.fori_loop(0, t, issue, 0)
    for _ in range(TOP_K):
        pltpu.make_async_copy(fp_ref, xs_hbm.at[pl.ds(0, t), :], sem_rows).wait()


def _dispatch(slot_tiles, fp, xs_zero):
    n, c = fp.shape
    t = ROUTE_TILE
    return pl.pallas_call(
        _dispatch_kernel,
        grid=(n // t,),
        in_specs=[
            pl.BlockSpec(memory_space=pl.ANY),
            pl.BlockSpec((t, c), lambda i: (i, 0)),
            pl.BlockSpec(memory_space=pl.ANY),
        ],
        out_specs=pl.BlockSpec(memory_space=pl.ANY),
        out_shape=jax.ShapeDtypeStruct(xs_zero.shape, U32),
        scratch_shapes=[pltpu.SMEM((TOP_K * t,), I32), pltpu.SemaphoreType.DMA, pltpu.SemaphoreType.DMA],
        input_output_aliases={2: 0},
        compiler_params=_params("arbitrary"),
        name="dispatch",
    )(slot_tiles, fp, xs_zero)


def _experts_kernel(be_ref, nt_ref, xs_ref, wg_ref, wu_ref, wd_ref, ys_ref):
    del be_ref
    used = pl.program_id(0) < nt_ref[0]

    @pl.when(jnp.logical_not(used))
    def _():
        ys_ref[...] = jnp.zeros(ys_ref.shape, ys_ref.dtype)

    @pl.when(used)
    def _():
        lo, hi = _unpack_halves(xs_ref[...])
        lo = lo.astype(BF16)
        hi = hi.astype(BF16)
        c = lo.shape[1]
        g = (jnp.dot(lo, wg_ref[0, 0:c, :], preferred_element_type=F32)
             + jnp.dot(hi, wg_ref[0, c:, :], preferred_element_type=F32))
        u = (jnp.dot(lo, wu_ref[0, 0:c, :], preferred_element_type=F32)
             + jnp.dot(hi, wu_ref[0, c:, :], preferred_element_type=F32))
        a = (g * _sigmoid(g) * u).astype(BF16)
        ys_ref[...] = _pack_halves(jnp.dot(a, wd_ref[0], preferred_element_type=F32))


def _experts(block_e, n_tiles, xs, wg, wu, wd):
    cap, c = xs.shape
    e, d, ff = wg.shape
    tm = EXPERT_TILE

    def row_map(b, be, nt):
        return (jnp.minimum(b, nt[0] - 1), 0)

    grid_spec = pltpu.PrefetchScalarGridSpec(
        num_scalar_prefetch=2,
        grid=(cap // tm,),
        in_specs=[
            pl.BlockSpec((tm, c), row_map),
            pl.BlockSpec((1, d, ff), lambda b, be, nt: (be[b], 0, 0)),
            pl.BlockSpec((1, d, ff), lambda b, be, nt: (be[b], 0, 0)),
            pl.BlockSpec((1, ff, d), lambda b, be, nt: (be[b], 0, 0)),
        ],
        out_specs=pl.BlockSpec((tm, c), lambda b, be, nt: (b, 0)),
    )
    return pl.pallas_call(
        _experts_kernel,
        grid_spec=grid_spec,
        out_shape=jax.ShapeDtypeStruct((cap, c), U32),
        compiler_params=_params("arbitrary"),
        name="experts",
    )(block_e, n_tiles, xs, wg, wu, wd)


def _combine_kernel(slot_hbm, h_ref, w_ref, mod_ref, g_ref, sg_ref, su_ref, sd_ref, ys_hbm, o_ref,
                    slot_smem, ybuf, sem_idx, sem_rows):
    i = pl.program_id(0)
    t = h_ref.shape[0]
    idx_copy = pltpu.make_async_copy(slot_hbm.at[i], slot_smem, sem_idx)
    idx_copy.start()
    idx_copy.wait()

    def issue(tok, carry):
        for k in range(TOP_K):
            s = slot_smem[k * t + tok]
            pltpu.make_async_copy(ys_hbm.at[pl.ds(s, 1), :], ybuf.at[k, pl.ds(tok, 1), :], sem_rows).start()
        return carry

    lax.fori_loop(0, t, issue, 0)
    h1 = h_ref[...]
    f = _modulate(h1, g_ref[0:1, :], mod_ref[0, 3:4, :], mod_ref[0, 4:5, :]).astype(BF16)
    sg = jnp.dot(f, sg_ref[...], preferred_element_type=F32)
    su = jnp.dot(f, su_ref[...], preferred_element_type=F32)
    y = jnp.dot((sg * _sigmoid(sg) * su).astype(BF16), sd_ref[...], preferred_element_type=F32)
    for k in range(TOP_K):
        pltpu.make_async_copy(ys_hbm.at[pl.ds(0, t), :], ybuf.at[k], sem_rows).wait()
    c = ybuf.shape[2]
    wt = w_ref[...].T
    y_lo = y[:, 0:c]
    y_hi = y[:, c:]
    for k in range(TOP_K):
        lo, hi = _unpack_halves(ybuf[k])
        wk = wt[:, k:k + 1]
        y_lo = y_lo + wk * lo
        y_hi = y_hi + wk * hi
    y = jnp.concatenate([y_lo, y_hi], axis=1)
    o_ref[...] = h1 + mod_ref[0, 5:6, :] * _rms(y, g_ref[1:2, :])


def _combine(slot_tiles, h1, w, mod, g23, sg, su, sd, ys, n_ctx):
    n, d = h1.shape
    t = ROUTE_TILE
    c = ys.shape[1]
    ff = sg.shape[1]
    ctx_tiles = n_ctx // t
    return pl.pallas_call(
        _combine_kernel,
        grid=(n // t,),
        in_specs=[
            pl.BlockSpec(memory_space=pl.ANY),
            pl.BlockSpec((t, d), lambda i: (i, 0)),
            pl.BlockSpec((TOP_K, t), lambda i: (0, i)),
            pl.BlockSpec((1, 6, d), lambda i: (jnp.where(i < ctx_tiles, 1, 0), 0, 0)),
            pl.BlockSpec((2, d), lambda i: (0, 0)),
            pl.BlockSpec((d, ff), lambda i: (0, 0)),
            pl.BlockSpec((d, ff), lambda i: (0, 0)),
            pl.BlockSpec((ff, d), lambda i: (0, 0)),
            pl.BlockSpec(memory_space=pl.ANY),
        ],
        out_specs=pl.BlockSpec((t, d), lambda i: (i, 0)),
        out_shape=jax.ShapeDtypeStruct((n, d), F32),
        scratch_shapes=[
            pltpu.SMEM((TOP_K * t,), I32),
            pltpu.VMEM((TOP_K, t, c), U32),
            pltpu.SemaphoreType.DMA,
            pltpu.SemaphoreType.DMA,
        ],
        compiler_params=_params("arbitrary"),
        name="combine",
    )(slot_tiles, h1, w, mod, g23, sg, su, sd, ys)


def _rope_tables(n_ctx, seq):
    nf = HEAD_DIM // 4
    inv = ROPE_THETA ** (-jnp.arange(nf, dtype=F32) / nf)
    pos = jnp.arange(seq, dtype=I32)
    row_ang = (pos // GRID_W).astype(F32)[:, None] * inv
    col_ang = (pos % GRID_W).astype(F32)[:, None] * inv
    cos = jnp.concatenate([jnp.cos(row_ang)] * 2 + [jnp.cos(col_ang)] * 2, axis=1)
    sin = jnp.concatenate([-jnp.sin(row_ang), jnp.sin(row_ang), -jnp.sin(col_ang), jnp.sin(col_ang)], axis=1)
    cos = jnp.concatenate([jnp.ones((n_ctx, HEAD_DIM), F32), cos], axis=0)
    sin = jnp.concatenate([jnp.zeros((n_ctx, HEAD_DIM), F32), sin], axis=0)
    return cos, sin


def _split_router(router):
    hi = router.astype(BF16)
    lo = (router - hi.astype(F32)).astype(BF16)
    return jnp.concatenate([hi, lo], axis=1)


def _moe(h1, fp, logits, mod, g23, bias, wg, wu, wd, sg, su, sd, n_ctx):
    n = h1.shape[0]
    t = ROUTE_TILE
    tm = EXPERT_TILE
    top_e, w, rank, cnt = _route(logits, bias.reshape(N_EXPERTS, 1))
    counts = cnt[:, 0].astype(I32)
    padded = (counts + tm - 1) // tm * tm
    pad_end = jnp.cumsum(padded)
    pad_start = pad_end - padded
    onehot = top_e[:, :, None] == jnp.arange(N_EXPERTS, dtype=I32)
    slot = rank + jnp.sum(jnp.where(onehot, pad_start, 0), axis=-1)
    slot_tiles = slot.reshape(TOP_K, n // t, t).transpose(1, 0, 2).reshape(n // t, TOP_K * t)
    cap = n * TOP_K + N_EXPERTS * tm
    nb = cap // tm
    block_e = jnp.searchsorted(pad_end, jnp.arange(nb, dtype=I32) * tm, side="right").astype(I32)
    n_tiles = (pad_end[-1] // tm).astype(I32).reshape(1)
    last_e = jnp.take(block_e, jnp.maximum(n_tiles[0] - 1, 0))
    block_e = jnp.where(jnp.arange(nb) < n_tiles[0], jnp.minimum(block_e, N_EXPERTS - 1), last_e).astype(I32)
    xs = _dispatch(slot_tiles, fp, jnp.zeros((cap, fp.shape[1]), U32))
    ys = _experts(block_e, n_tiles, xs, wg, wu, wd)
    return _combine(slot_tiles, h1, w, mod, g23, sg, su, sd, ys, n_ctx)


def kernel(x, c, ctx, c_ctx, ada_w, ada_b, norm_g, ev_w_in, ev_w_out, ev_q_norm, ev_k_norm, ev_dw_w, ev_dw_b,
           ev_ln_g, ev_ln_b, od_w_in, od_w_out, od_lambda, od_subln_g, moe_router, moe_bias, moe_w_gate,
           moe_w_up, moe_w_down, sh_w_gate, sh_w_up, sh_w_down):
    b, seq, d = x.shape
    n_ctx = ctx.shape[1]
    assert b == 1 and n_ctx % ROW_TILE == 0 and seq % ROW_TILE == 0 and seq % GRID_W == 0
    depth = ada_w.shape[0]
    assert depth == 2
    cos, sin = _rope_tables(n_ctx, seq)
    cvec = jnp.stack([c[0], c_ctx], axis=1)
    mod_all = _ada(cvec, ada_w, ada_b).reshape(depth, 2, 6, d)
    h = jnp.concatenate([ctx[0], x[0]], axis=0)
    qscale = math.log2(math.e) * HEAD_DIM ** -0.5
    c_conv = ev_dw_w.shape[-1]

    mod = mod_all[0]
    gains = norm_g[0]
    nq = A_HEADS * HEAD_DIM
    nkv = 2 * A_KV_HEADS * HEAD_DIM
    w_in = jnp.concatenate([ev_w_in[0][:, :nq], ev_w_in[0][:, nq + nkv:], ev_w_in[0][:, nq:nq + nkv]], axis=1)
    qkvu = _modmm(h, mod, gains[0:1], w_in.astype(BF16), n_ctx, col_tile=w_in.shape[1] // 2)
    u_col0 = nq
    k_col0 = nq + 2 * c_conv
    v_col0 = k_col0 + A_KV_HEADS * HEAD_DIM
    q = _prep_q(qkvu, 0, A_HEADS, cos, sin, ev_q_norm[0].reshape(1, HEAD_DIM), True, qscale)
    kt = _prep_k(qkvu, k_col0, A_KV_HEADS, cos, sin, ev_k_norm[0].reshape(1, HEAD_DIM), True)
    o_attn = _gqa_attention(q, kt, qkvu, v_col0, n_ctx)
    o_conv = _conformer_conv(qkvu, u_col0, c_conv, ev_dw_w[0], ev_dw_b[0].reshape(1, -1),
                             ev_ln_g[0].reshape(1, -1), ev_ln_b[0].reshape(1, -1), n_ctx)
    h1, fp, logits = _post([o_attn, o_conv], ev_w_out[0].astype(BF16), h, 0, mod, gains[1:3],
                           _split_router(moe_router[0]), n_ctx)
    h = _moe(h1, fp, logits, mod, gains[2:4], moe_bias[0], moe_w_gate[0].astype(BF16), moe_w_up[0].astype(BF16),
             moe_w_down[0].astype(BF16), sh_w_gate[0].astype(BF16), sh_w_up[0].astype(BF16),
             sh_w_down[0].astype(BF16), n_ctx)

    mod = mod_all[1]
    gains = norm_g[1]
    lam_init = 0.8 - 0.6 * math.exp(-0.3 * 1)
    lp = od_lambda[0].astype(F32)
    lam = (jnp.exp(jnp.sum(lp[0] * lp[1])) - jnp.exp(jnp.sum(lp[2] * lp[3])) + lam_init).reshape(1)
    qkv = _modmm(h, mod, gains[0:1], od_w_in[0].astype(BF16), n_ctx, col_tile=od_w_in.shape[2] // 4)
    n_qk = 2 * DIFF_HEADS
    ones = jnp.ones((1, HEAD_DIM), F32)
    q = _prep_q(qkv, 0, n_qk, cos, sin, ones, False, qscale)
    kt = _prep_k(qkv, n_qk * HEAD_DIM, n_qk, cos, sin, ones, False)
    o_attn = _diff_attention(lam, q, kt, qkv, 2 * n_qk * HEAD_DIM, od_subln_g[0].reshape(1, -1),
                             1.0 - lam_init, n_ctx)
    h1, fp, logits = _post([o_attn], od_w_out[0].astype(BF16), h, n_ctx, mod, gains[1:3],
                           _split_router(moe_router[1]), n_ctx)
    h = _moe(h1, fp, logits, mod, gains[2:4], moe_bias[1], moe_w_gate[1].astype(BF16), moe_w_up[1].astype(BF16),
             moe_w_down[1].astype(BF16), sh_w_gate[1].astype(BF16), sh_w_up[1].astype(BF16),
             sh_w_down[1].astype(BF16), 0)
    return h[None]
```

```python
import functools
import math

import jax
import jax.numpy as jnp
from jax import lax
from jax.experimental import pallas as pl
from jax.experimental.pallas import tpu as pltpu

F32 = jnp.float32
BF16 = jnp.bfloat16
U32 = jnp.uint32
I32 = jnp.int32

GRID_W = 64
HEAD_DIM = 128
ROPE_THETA = 10000.0
EPS = 1e-6
A_HEADS = 8
A_KV_HEADS = 2
CONV_W = 31
DIFF_HEADS = 8
N_EXPERTS = 64
TOP_K = 8
N_GROUPS = 8
TOPK_GROUPS = 4
ROUTED_SCALE = 2.5

LANES = 128
SUBLANES = 8
VMEM_LIMIT_BYTES = 56 * 1024 * 1024

ROW_TILE = 256
KV_CHUNK = 256
LAT_SUB = 4
Q_TILE_GQA = 128
Q_TILE_DIFF = 512
ROUTE_TILE = 256
EXPERT_TILE = 256
CONV_HALO = 16
NEG_BIG = -1e30
FAST_BOUND_MAX = 60.0


def _params(*sem):
    return pltpu.CompilerParams(dimension_semantics=sem, vmem_limit_bytes=VMEM_LIMIT_BYTES)


def _sigmoid(x):
    return 1.0 / (1.0 + jnp.exp(-x))


def _rms(x, g):
    return x * lax.rsqrt(jnp.mean(x * x, axis=-1, keepdims=True) + EPS) * g


def _modulate(x, g, shift, scale):
    return _rms(x, g) * (1.0 + scale) + shift


def _pack_halves(x):
    c = x.shape[1] // 2
    lo = lax.bitcast_convert_type(x[:, :c].astype(BF16).astype(F32), U32)
    hi = lax.bitcast_convert_type(x[:, c:].astype(BF16).astype(F32), U32)
    return (hi & jnp.uint32(0xFFFF0000)) | (lo >> 16)


def _unpack_halves(w):
    lo = lax.bitcast_convert_type(w << 16, F32)
    hi = lax.bitcast_convert_type(w & jnp.uint32(0xFFFF0000), F32)
    return lo, hi


ADA_COLS = 512
ADA_KCHUNK = 64


def _ada_kernel(cv_ref, w_ref, b_ref, o_ref, s_ref):
    cv = cv_ref[...]
    s_ref[...] = cv * _sigmoid(cv)
    d = cv.shape[0]
    tn = o_ref.shape[-1]

    def body(k, acc):
        a0, a1 = acc
        r0 = pl.multiple_of(k * ADA_KCHUNK, ADA_KCHUNK)
        w = w_ref[0, pl.ds(r0, ADA_KCHUNK), :]
        s = s_ref[pl.ds(r0, ADA_KCHUNK), :]
        p0 = (w * s[:, 0:1]).reshape(ADA_KCHUNK // SUBLANES, SUBLANES, tn)
        p1 = (w * s[:, 1:2]).reshape(ADA_KCHUNK // SUBLANES, SUBLANES, tn)
        return a0 + jnp.sum(p0, axis=0), a1 + jnp.sum(p1, axis=0)

    z = jnp.zeros((SUBLANES, tn), F32)
    a0, a1 = lax.fori_loop(0, d // ADA_KCHUNK, body, (z, z))
    b = b_ref[0]
    o_ref[0, 0:1, :] = jnp.sum(a0, axis=0, keepdims=True) + b
    o_ref[0, 1:2, :] = jnp.sum(a1, axis=0, keepdims=True) + b


def _ada(cvec, ada_w, ada_b):
    depth, d, n = ada_w.shape
    return pl.pallas_call(
        _ada_kernel,
        grid=(depth, n // ADA_COLS),
        in_specs=[
            pl.BlockSpec((d, 2), lambda l, j: (0, 0)),
            pl.BlockSpec((1, d, ADA_COLS), lambda l, j: (l, 0, j)),
            pl.BlockSpec((1, 1, ADA_COLS), lambda l, j: (l, 0, j)),
        ],
        out_specs=pl.BlockSpec((1, 2, ADA_COLS), lambda l, j: (l, 0, j)),
        out_shape=jax.ShapeDtypeStruct((depth, 2, n), F32),
        scratch_shapes=[pltpu.VMEM((d, 2), F32)],
        compiler_params=_params("arbitrary", "arbitrary"),
        name="ada",
    )(cvec, ada_w, ada_b.reshape(depth, 1, n))


def _modmm_kernel(h_ref, mod_ref, g_ref, w_ref, o_ref):
    a = _modulate(h_ref[...], g_ref[...], mod_ref[0, 0:1, :], mod_ref[0, 1:2, :])
    o_ref[...] = jnp.dot(a.astype(BF16), w_ref[...], preferred_element_type=F32).astype(o_ref.dtype)


def _modmm(h, mod, g, w, n_ctx, col_tile):
    r, d = h.shape
    n = w.shape[1]
    ctx_tiles = n_ctx // ROW_TILE
    return pl.pallas_call(
        _modmm_kernel,
        grid=(n // col_tile, r // ROW_TILE),
        in_specs=[
            pl.BlockSpec((ROW_TILE, d), lambda j, i: (i, 0)),
            pl.BlockSpec((1, 6, d), lambda j, i: (jnp.where(i < ctx_tiles, 1, 0), 0, 0)),
            pl.BlockSpec((1, d), lambda j, i: (0, 0)),
            pl.BlockSpec((d, col_tile), lambda j, i: (0, j)),
        ],
        out_specs=pl.BlockSpec((ROW_TILE, col_tile), lambda j, i: (i, j)),
        out_shape=jax.ShapeDtypeStruct((r, n), BF16),
        compiler_params=_params("arbitrary", "arbitrary"),
        name="modmm",
    )(h, mod, g, w)


def _rope(x, c, s):
    lane = lax.broadcasted_iota(I32, x.shape, 1)
    first = (lane % 64) < 32
    partner = jnp.where(first, pltpu.roll(x, 96, axis=1), pltpu.roll(x, 32, axis=1))
    return x * c + partner * s


def _prep_kernel(x_ref, c_ref, s_ref, g_ref, o_ref, st_ref, *, n_heads, norm, scale, transpose):
    c = c_ref[...]
    s = s_ref[...]
    stats = []
    for h in range(n_heads):
        x = x_ref[:, h * HEAD_DIM:(h + 1) * HEAD_DIM].astype(F32)
        if norm:
            x = _rms(x, g_ref[...])
        y = (_rope(x, c, s) * scale).astype(BF16).astype(F32)
        sq = jnp.max(jnp.sum(y * y, axis=-1, keepdims=True), axis=0, keepdims=True)
        stats.append(jnp.broadcast_to(sq, (1, LANES)))
        if transpose:
            o_ref[h, 0] = y.T.astype(o_ref.dtype)
        else:
            o_ref[:, h * HEAD_DIM:(h + 1) * HEAD_DIM] = y.astype(o_ref.dtype)
    st_ref[0] = jnp.concatenate(stats, axis=0)


def _prep(x, row0, col0, n_heads, cos, sin, g, norm, scale, transpose):
    r = x.shape[0] - row0
    width = n_heads * HEAD_DIM
    cb = col0 // width
    tr = KV_CHUNK
    nt = r // tr
    rb = row0 // tr
    if transpose:
        o_spec = pl.BlockSpec((n_heads, 1, HEAD_DIM, tr), lambda i: (0, i, 0, 0))
        o_shape = jax.ShapeDtypeStruct((n_heads, nt, HEAD_DIM, tr), BF16)
    else:
        o_spec = pl.BlockSpec((tr, width), lambda i: (i, 0))
        o_shape = jax.ShapeDtypeStruct((r, width), BF16)
    out, stats = pl.pallas_call(
        functools.partial(_prep_kernel, n_heads=n_heads, norm=norm, scale=scale, transpose=transpose),
        grid=(nt,),
        in_specs=[
            pl.BlockSpec((tr, width), lambda i: (i + rb, cb)),
            pl.BlockSpec((tr, HEAD_DIM), lambda i: (i + rb, 0)),
            pl.BlockSpec((tr, HEAD_DIM), lambda i: (i + rb, 0)),
            pl.BlockSpec((1, HEAD_DIM), lambda i: (0, 0)),
        ],
        out_specs=[o_spec, pl.BlockSpec((1, n_heads, LANES), lambda i: (i, 0, 0))],
        out_shape=[o_shape, jax.ShapeDtypeStruct((nt, n_heads, LANES), F32)],
        compiler_params=_params("arbitrary"),
        name="prep_k" if transpose else "prep_q",
    )(x, cos, sin, g)
    return out, jnp.sqrt(jnp.max(stats[:, :, 0], axis=0))


def _lane_tile(x, n):
    return x if n == 1 else jnp.concatenate([x] * n, axis=1)


def _lane_fold(p):
    out = p[:, 0:LANES]
    for j in range(1, p.shape[1] // LANES):
        out = out + p[:, j * LANES:(j + 1) * LANES]
    return out


def _softmax_step(q, kt, v, m_ref, l_ref, acc_ref, online):
    s = jnp.dot(q, kt, preferred_element_type=F32)
    tiles = s.shape[1] // LANES
    if online:
        m_prev = m_ref[...]
        m_new = jnp.maximum(m_prev, jnp.max(s, axis=-1, keepdims=True))
        alpha = jnp.exp2(m_prev - m_new)
        p = jnp.exp2(s - _lane_tile(m_new, tiles))
        l_ref[...] = alpha * l_ref[...] + _lane_fold(p)
        acc_ref[...] = (_lane_tile(alpha, acc_ref.shape[1] // LANES) * acc_ref[...]
                        + jnp.dot(p.astype(BF16), v, preferred_element_type=F32))
        m_ref[...] = m_new
    else:
        p = jnp.exp2(s - _lane_tile(m_ref[...], tiles))
        l_ref[...] += _lane_fold(p)
        acc_ref[...] += jnp.dot(p.astype(BF16), v, preferred_element_type=F32)


def _scan_keys(kt_ref, n_k, v_ref, n_ctx, n_lat, see_latent, step):
    for cc in range(n_ctx // KV_CHUNK):
        step([kt_ref[j, cc] for j in range(n_k)], v_ref[cc * KV_CHUNK:(cc + 1) * KV_CHUNK, :])
    ctx_chunks = n_ctx // KV_CHUNK

    def latent():
        def body(c, carry):
            base = ctx_chunks + c * LAT_SUB
            kts = [jnp.concatenate([kt_ref[j, base + u] for u in range(LAT_SUB)], axis=1) for j in range(n_k)]
            r0 = pl.multiple_of(n_ctx + c * (LAT_SUB * KV_CHUNK), KV_CHUNK)
            step(kts, v_ref[pl.ds(r0, LAT_SUB * KV_CHUNK), :])
            return carry

        lax.fori_loop(0, n_lat, body, 0)

    if see_latent is None:
        latent()
    else:
        pl.when(see_latent)(latent)


def _row_bound(q, kmax):
    qf = q.astype(F32)
    return jnp.sqrt(jnp.sum(qf * qf, axis=-1, keepdims=True)) * kmax


def _gqa_kernel(fast_ref, kmax_ref, q_ref, kt_ref, v_ref, o_ref, qs_ref, m_ref, l_ref, acc_ref,
                *, ctx_tiles, n_ctx, n_lat):
    g = pl.program_id(0)
    tq = q_ref.shape[0]
    group = q_ref.shape[1] // HEAD_DIM
    for u in range(group):
        qs_ref[u * tq:(u + 1) * tq, :] = q_ref[:, u * HEAD_DIM:(u + 1) * HEAD_DIM]
    l_ref[...] = jnp.zeros(l_ref.shape, F32)
    acc_ref[...] = jnp.zeros(acc_ref.shape, F32)
    see_latent = pl.program_id(1) >= ctx_tiles
    fast = fast_ref[g] == 1

    def scan(online):
        def step(kts, v):
            _softmax_step(qs_ref[...], kts[0], v, m_ref, l_ref, acc_ref, online)

        _scan_keys(kt_ref, 1, v_ref, n_ctx, n_lat, see_latent, step)

    @pl.when(fast)
    def _():
        m_ref[...] = jnp.broadcast_to(_row_bound(qs_ref[...], kmax_ref[g]), m_ref.shape)
        scan(False)

    @pl.when(jnp.logical_not(fast))
    def _():
        m_ref[...] = jnp.full(m_ref.shape, NEG_BIG, F32)
        scan(True)

    o = acc_ref[...] / jnp.sum(l_ref[...], axis=-1, keepdims=True)
    for u in range(group):
        o_ref[:, u * HEAD_DIM:(u + 1) * HEAD_DIM] = o[u * tq:(u + 1) * tq, :].astype(o_ref.dtype)


def _kv_chunks(r, n_ctx):
    n_lat_chunks = (r - n_ctx) // KV_CHUNK
    assert n_ctx % KV_CHUNK == 0 and n_lat_chunks % LAT_SUB == 0
    return n_lat_chunks // LAT_SUB


def _gqa_attention(q, qmax, kt, kmax, qkvu, v_col0, n_ctx):
    r = q.shape[0]
    group = A_HEADS // A_KV_HEADS
    tq = Q_TILE_GQA
    nc = r // KV_CHUNK
    vb = v_col0 // HEAD_DIM
    fast = (jnp.max(qmax.reshape(A_KV_HEADS, group), axis=1) * kmax <= FAST_BOUND_MAX).astype(I32)
    kern = functools.partial(_gqa_kernel, ctx_tiles=n_ctx // tq, n_ctx=n_ctx, n_lat=_kv_chunks(r, n_ctx))
    smem = pl.BlockSpec(memory_space=pltpu.SMEM)
    return pl.pallas_call(
        kern,
        grid=(A_KV_HEADS, r // tq),
        in_specs=[
            smem, smem,
            pl.BlockSpec((tq, group * HEAD_DIM), lambda g, i: (i, g)),
            pl.BlockSpec((1, nc, HEAD_DIM, KV_CHUNK), lambda g, i: (g, 0, 0, 0)),
            pl.BlockSpec((r, HEAD_DIM), lambda g, i: (0, vb + g)),
        ],
        out_specs=pl.BlockSpec((tq, group * HEAD_DIM), lambda g, i: (i, g)),
        out_shape=jax.ShapeDtypeStruct((r, A_HEADS * HEAD_DIM), BF16),
        scratch_shapes=[
            pltpu.VMEM((group * tq, HEAD_DIM), BF16),
            pltpu.VMEM((group * tq, LANES), F32),
            pltpu.VMEM((group * tq, LANES), F32),
            pltpu.VMEM((group * tq, HEAD_DIM), F32),
        ],
        compiler_params=_params("arbitrary", "arbitrary"),
        name="gqa_attn",
    )(fast, kmax, q, kt, qkvu)


def _diff_kernel(lam_ref, fast_ref, kmax_ref, q_ref, kt_ref, v_ref, g_ref, o_ref, m0, l0, a0, m1, l1, a1,
                 *, out_scale, n_ctx, n_lat):
    h = pl.program_id(0)
    comps = ((m0, l0, a0), (m1, l1, a1))
    for _, l_ref, a_ref in comps:
        l_ref[...] = jnp.zeros(l_ref.shape, F32)
        a_ref[...] = jnp.zeros(a_ref.shape, F32)
    fast = fast_ref[h] == 1

    def q_comp(c):
        return q_ref[:, c * HEAD_DIM:(c + 1) * HEAD_DIM]

    def scan(online):
        def step(kts, v):
            for c, (m_ref, l_ref, a_ref) in enumerate(comps):
                _softmax_step(q_comp(c), kts[c], v, m_ref, l_ref, a_ref, online)

        _scan_keys(kt_ref, 2, v_ref, n_ctx, n_lat, None, step)

    @pl.when(fast)
    def _():
        for c, (m_ref, _, _) in enumerate(comps):
            m_ref[...] = jnp.broadcast_to(_row_bound(q_comp(c), kmax_ref[2 * h + c]), m_ref.shape)
        scan(False)

    @pl.when(jnp.logical_not(fast))
    def _():
        for m_ref, _, _ in comps:
            m_ref[...] = jnp.full(m_ref.shape, NEG_BIG, F32)
        scan(True)

    o = (a0[...] / jnp.sum(l0[...], axis=-1, keepdims=True)
         - lam_ref[0] * (a1[...] / jnp.sum(l1[...], axis=-1, keepdims=True)))
    o_ref[...] = (_rms(o, g_ref[...]) * out_scale).astype(o_ref.dtype)


def _diff_attention(lam, q, qmax, kt, kmax, qkv, v_col0, subln_g, out_scale, n_ctx):
    r = qkv.shape[0]
    seq = q.shape[0]
    tq = math.gcd(Q_TILE_DIFF, seq)
    nc = r // KV_CHUNK
    dv = 2 * HEAD_DIM
    vb = v_col0 // dv
    fast = (jnp.max((qmax * kmax).reshape(DIFF_HEADS, 2), axis=1) <= FAST_BOUND_MAX).astype(I32)
    kern = functools.partial(_diff_kernel, out_scale=out_scale, n_ctx=n_ctx, n_lat=_kv_chunks(r, n_ctx))
    stat = pltpu.VMEM((tq, LANES), F32)
    acc = pltpu.VMEM((tq, dv), F32)
    smem = pl.BlockSpec(memory_space=pltpu.SMEM)
    return pl.pallas_call(
        kern,
        grid=(DIFF_HEADS, seq // tq),
        in_specs=[
            smem, smem, smem,
            pl.BlockSpec((tq, dv), lambda h, i: (i, h)),
            pl.BlockSpec((2, nc, HEAD_DIM, KV_CHUNK), lambda h, i: (h, 0, 0, 0), pipeline_mode=pl.Buffered(1)),
            pl.BlockSpec((r, dv), lambda h, i: (0, vb + h), pipeline_mode=pl.Buffered(1)),
            pl.BlockSpec((1, dv), lambda h, i: (0, 0)),
        ],
        out_specs=pl.BlockSpec((tq, dv), lambda h, i: (i, h)),
        out_shape=jax.ShapeDtypeStruct((seq, DIFF_HEADS * dv), BF16),
        scratch_shapes=[stat, stat, acc, stat, stat, acc],
        compiler_params=_params("arbitrary", "arbitrary"),
        name="diff_attn",
    )(lam, fast, kmax, q, kt, qkv, subln_g)


def _conv_kernel(ap_ref, gp_ref, ac_ref, gc_ref, an_ref, gn_ref, w_ref, b_ref, lg_ref, lb_ref, o_ref, h_ref,
                 *, seq_starts, seq_ends):
    i = pl.program_id(0)
    tr = ac_ref.shape[0]
    is_start = functools.reduce(jnp.logical_or, [i == s for s in seq_starts])
    is_end = functools.reduce(jnp.logical_or, [i == e for e in seq_ends])

    def glu(a_ref, g_ref):
        return a_ref[...].astype(F32) * _sigmoid(g_ref[...].astype(F32))

    h_ref[0:CONV_HALO, :] = jnp.where(is_start, 0.0, glu(ap_ref, gp_ref))
    h_ref[CONV_HALO:CONV_HALO + tr, :] = glu(ac_ref, gc_ref)
    h_ref[CONV_HALO + tr:, :] = jnp.where(is_end, 0.0, glu(an_ref, gn_ref))
    acc = jnp.zeros((tr, ac_ref.shape[1]), F32)
    base = CONV_HALO - CONV_W // 2
    for j in range(CONV_W):
        acc = acc + w_ref[j:j + 1, :] * h_ref[base + j:base + j + tr, :]
    y = acc + b_ref[...]
    mu = jnp.mean(y, axis=-1, keepdims=True)
    var = jnp.mean(jnp.square(y - mu), axis=-1, keepdims=True)
    z = (y - mu) * lax.rsqrt(var + EPS) * lg_ref[...] + lb_ref[...]
    o_ref[...] = (z * _sigmoid(z)).astype(o_ref.dtype)


CONV_TILE = 128


def _conformer_conv(qkvu, u_col0, c_conv, dw_w, dw_b, ln_g, ln_b, n_ctx):
    r = qkvu.shape[0]
    tr = CONV_TILE
    nt = r // tr
    hb = tr // CONV_HALO
    ab = u_col0 // c_conv
    gb = ab + 1
    last_halo = r // CONV_HALO - 1
    seq_starts = (0, n_ctx // tr)
    seq_ends = (n_ctx // tr - 1, nt - 1)

    def prev(col):
        return pl.BlockSpec((CONV_HALO, c_conv), lambda i: (jnp.maximum(i * hb - 1, 0), col))

    def cur(col):
        return pl.BlockSpec((tr, c_conv), lambda i: (i, col))

    def nxt(col):
        return pl.BlockSpec((CONV_HALO, c_conv), lambda i: (jnp.minimum((i + 1) * hb, last_halo), col))

    vec = pl.BlockSpec((1, c_conv), lambda i: (0, 0))
    return pl.pallas_call(
        functools.partial(_conv_kernel, seq_starts=seq_starts, seq_ends=seq_ends),
        grid=(nt,),
        in_specs=[prev(ab), prev(gb), cur(ab), cur(gb), nxt(ab), nxt(gb),
                  pl.BlockSpec((CONV_W, c_conv), lambda i: (0, 0)), vec, vec, vec],
        out_specs=pl.BlockSpec((tr, c_conv), lambda i: (i, 0)),
        out_shape=jax.ShapeDtypeStruct((r, c_conv), BF16),
        scratch_shapes=[pltpu.VMEM((tr + 2 * CONV_HALO, c_conv), F32)],
        compiler_params=_params("arbitrary"),
        name="conformer_conv",
    )(qkvu, qkvu, qkvu, qkvu, qkvu, qkvu, dw_w, dw_b, ln_g, ln_b)


def _post_kernel(*refs, n_in):
    o_refs = refs[:n_in]
    w_ref, h_ref, mod_ref, g_ref, rr_ref, h1_ref, fp_ref, lg_ref = refs[n_in:]
    k0 = 0
    out = None
    for o_in in o_refs:
        kw = o_in.shape[1]
        part = jnp.dot(o_in[...], w_ref[k0:k0 + kw, :], preferred_element_type=F32)
        out = part if out is None else out + part
        k0 += kw
    h1 = h_ref[...] + mod_ref[0, 2:3, :] * _rms(out, g_ref[0:1, :])
    h1_ref[...] = h1
    f = _modulate(h1, g_ref[1:2, :], mod_ref[0, 3:4, :], mod_ref[0, 4:5, :])
    fp_ref[...] = _pack_halves(f)
    f_hi = f.astype(BF16)
    f_lo = (f - f_hi.astype(F32)).astype(BF16)
    l2 = (jnp.dot(f_hi, rr_ref[...], preferred_element_type=F32)
          + jnp.dot(f_lo, rr_ref[...], preferred_element_type=F32))
    lg_ref[...] = l2 + pltpu.roll(l2, N_EXPERTS, axis=1)


def _post(o_parts, w_out, h, row_off, mod, g12, rr, n_ctx):
    n = o_parts[0].shape[0]
    d = h.shape[1]
    tm = ROW_TILE
    off = row_off // tm
    ctx_tiles = (n_ctx - row_off) // tm
    in_specs = [pl.BlockSpec((tm, o.shape[1]), lambda i: (i, 0)) for o in o_parts]
    in_specs += [
        pl.BlockSpec((d, d), lambda i: (0, 0)),
        pl.BlockSpec((tm, d), lambda i: (i + off, 0)),
        pl.BlockSpec((1, 6, d), lambda i: (jnp.where(i < ctx_tiles, 1, 0), 0, 0)),
        pl.BlockSpec((2, d), lambda i: (0, 0)),
        pl.BlockSpec((d, 2 * N_EXPERTS), lambda i: (0, 0)),
    ]
    return pl.pallas_call(
        functools.partial(_post_kernel, n_in=len(o_parts)),
        grid=(n // tm,),
        in_specs=in_specs,
        out_specs=[
            pl.BlockSpec((tm, d), lambda i: (i, 0)),
            pl.BlockSpec((tm, d // 2), lambda i: (i, 0)),
            pl.BlockSpec((tm, 2 * N_EXPERTS), lambda i: (i, 0)),
        ],
        out_shape=[
            jax.ShapeDtypeStruct((n, d), F32),
            jax.ShapeDtypeStruct((n, d // 2), U32),
            jax.ShapeDtypeStruct((n, 2 * N_EXPERTS), F32),
        ],
        compiler_params=_params("arbitrary"),
        name="post_mixer",
    )(*o_parts, w_out, h, mod, g12, rr)


def _route_kernel(lg_ref, bias_ref, e_ref, w_ref, rank_ref, cnt_ref):
    t = lg_ref.shape[0]
    gsz = N_EXPERTS // N_GROUPS

    @pl.when(pl.program_id(0) == 0)
    def _():
        cnt_ref[...] = jnp.zeros(cnt_ref.shape, F32)

    logits = lg_ref[...].T[0:N_EXPERTS, :]
    scores = _sigmoid(logits)
    choice = scores + bias_ref[...]
    sub = lax.broadcasted_iota(I32, (gsz, t), 0).astype(F32)
    grp_rows = []
    for g in range(N_GROUPS):
        cg = choice[g * gsz:(g + 1) * gsz, :]
        m1 = jnp.max(cg, axis=0, keepdims=True)
        i1 = jnp.min(jnp.where(cg == m1, sub, float(gsz)), axis=0, keepdims=True)
        m2 = jnp.max(jnp.where(sub == i1, -jnp.inf, cg), axis=0, keepdims=True)
        grp_rows.append(m1 + m2)
    grp = jnp.concatenate(grp_rows, axis=0)
    gidx = lax.broadcasted_iota(I32, (N_GROUPS, t), 0).astype(F32)
    gsel = jnp.zeros((N_GROUPS, t), F32)
    for _ in range(TOPK_GROUPS):
        gm = jnp.max(grp, axis=0, keepdims=True)
        gi = jnp.min(jnp.where(grp == gm, gidx, float(N_GROUPS)), axis=0, keepdims=True)
        hit = gidx == gi
        gsel = jnp.where(hit, 1.0, gsel)
        grp = jnp.where(hit, -jnp.inf, grp)
    eidx = lax.broadcasted_iota(I32, (N_EXPERTS, t), 0).astype(F32)
    mc_rows = []
    for g in range(N_GROUPS):
        mc_rows.append(jnp.where(gsel[g:g + 1, :] > 0.5, choice[g * gsz:(g + 1) * gsz, :], -jnp.inf))
    mc = jnp.concatenate(mc_rows, axis=0)
    sel_all = jnp.zeros((N_EXPERTS, t), F32)
    e_rows, w_rows = [], []
    for _ in range(TOP_K):
        mx = jnp.max(mc, axis=0, keepdims=True)
        ei = jnp.min(jnp.where(mc == mx, eidx, float(N_EXPERTS)), axis=0, keepdims=True)
        hit = eidx == ei
        e_rows.append(ei)
        w_rows.append(jnp.sum(jnp.where(hit, scores, 0.0), axis=0, keepdims=True))
        sel_all = jnp.where(hit, 1.0, sel_all)
        mc = jnp.where(hit, -jnp.inf, mc)
    e_all = jnp.concatenate(e_rows, axis=0)
    w_all = jnp.concatenate(w_rows, axis=0)
    w_ref[...] = w_all / jnp.sum(w_all, axis=0, keepdims=True) * ROUTED_SCALE
    e_ref[...] = e_all.astype(I32)
    src = lax.broadcasted_iota(I32, (t, t), 0)
    dst = lax.broadcasted_iota(I32, (t, t), 1)
    upper = jnp.where(src < dst, 1.0, 0.0).astype(BF16)
    pos = jnp.dot(sel_all.astype(BF16), upper, preferred_element_type=F32) + cnt_ref[:, 0:1]
    r_rows = [jnp.sum(jnp.where(eidx == e_rows[k], pos, 0.0), axis=0, keepdims=True) for k in range(TOP_K)]
    rank_ref[...] = jnp.concatenate(r_rows, axis=0).astype(I32)
    cnt_ref[...] = cnt_ref[...] + jnp.sum(sel_all, axis=1, keepdims=True)


def _route(logits, bias):
    n = logits.shape[0]
    t = ROUTE_TILE
    row = pl.BlockSpec((TOP_K, t), lambda i: (0, i))
    return pl.pallas_call(
        _route_kernel,
        grid=(n // t,),
        in_specs=[
            pl.BlockSpec((t, 2 * N_EXPERTS), lambda i: (i, 0)),
            pl.BlockSpec((N_EXPERTS, 1), lambda i: (0, 0)),
        ],
        out_specs=[row, row, row, pl.BlockSpec((N_EXPERTS, LANES), lambda i: (0, 0))],
        out_shape=[
            jax.ShapeDtypeStruct((TOP_K, n), I32),
            jax.ShapeDtypeStruct((TOP_K, n), F32),
            jax.ShapeDtypeStruct((TOP_K, n), I32),
            jax.ShapeDtypeStruct((N_EXPERTS, LANES), F32),
        ],
        compiler_params=_params("arbitrary"),
        name="route",
    )(logits, bias)


def _dispatch_kernel(slot_hbm, fp_ref, xs_in, xs_hbm, slot_smem, sem_idx, sem_rows):
    del xs_in
    i = pl.program_id(0)
    t = fp_ref.shape[0]
    idx_copy = pltpu.make_async_copy(slot_hbm.at[i], slot_smem, sem_idx)
    idx_copy.start()
    idx_copy.wait()

    def issue(tok, carry):
        for k in range(TOP_K):
            s = slot_smem[k * t + tok]
            pltpu.make_async_copy(fp_ref.at[pl.ds(tok, 1), :], xs_hbm.at[pl.ds(s, 1), :], sem_rows).start()
        return carry

    lax.fori_loop(0, t, issue, 0)
    for _ in range(TOP_K):
        pltpu.make_async_copy(fp_ref, xs_hbm.at[pl.ds(0, t), :], sem_rows).wait()


def _dispatch(slot_tiles, fp, xs_zero):
    n, c = fp.shape
    t = ROUTE_TILE
    return pl.pallas_call(
        _dispatch_kernel,
        grid=(n // t,),
        in_specs=[
            pl.BlockSpec(memory_space=pl.ANY),
            pl.BlockSpec((t, c), lambda i: (i, 0)),
            pl.BlockSpec(memory_space=pl.ANY),
        ],
        out_specs=pl.BlockSpec(memory_space=pl.ANY),
        out_shape=jax.ShapeDtypeStruct(xs_zero.shape, U32),
        scratch_shapes=[pltpu.SMEM((TOP_K * t,), I32), pltpu.SemaphoreType.DMA, pltpu.SemaphoreType.DMA],
        input_output_aliases={2: 0},
        compiler_params=_params("arbitrary"),
        name="dispatch",
    )(slot_tiles, fp, xs_zero)


def _experts_kernel(be_ref, nt_ref, xs_ref, wg_ref, wu_ref, wd_ref, ys_ref):
    del be_ref
    used = pl.program_id(0) < nt_ref[0]

    @pl.when(jnp.logical_not(used))
    def _():
        ys_ref[...] = jnp.zeros(ys_ref.shape, ys_ref.dtype)

    @pl.when(used)
    def _():
        lo, hi = _unpack_halves(xs_ref[...])
        lo = lo.astype(BF16)
        hi = hi.astype(BF16)
        c = lo.shape[1]
        g = (jnp.dot(lo, wg_ref[0, 0, 0:c, :], preferred_element_type=F32)
             + jnp.dot(hi, wg_ref[0, 0, c:, :], preferred_element_type=F32))
        u = (jnp.dot(lo, wu_ref[0, 0, 0:c, :], preferred_element_type=F32)
             + jnp.dot(hi, wu_ref[0, 0, c:, :], preferred_element_type=F32))
        a = (g * _sigmoid(g) * u).astype(BF16)
        ys_ref[...] = _pack_halves(jnp.dot(a, wd_ref[0, 0], preferred_element_type=F32))


def _experts(layer, block_e, n_tiles, xs, wg, wu, wd):
    cap, c = xs.shape
    _, e, d, ff = wg.shape
    tm = EXPERT_TILE

    def row_map(b, be, nt):
        return (jnp.minimum(b, nt[0] - 1), 0)

    grid_spec = pltpu.PrefetchScalarGridSpec(
        num_scalar_prefetch=2,
        grid=(cap // tm,),
        in_specs=[
            pl.BlockSpec((tm, c), row_map),
            pl.BlockSpec((1, 1, d, ff), lambda b, be, nt: (layer, be[b], 0, 0)),
            pl.BlockSpec((1, 1, d, ff), lambda b, be, nt: (layer, be[b], 0, 0)),
            pl.BlockSpec((1, 1, ff, d), lambda b, be, nt: (layer, be[b], 0, 0)),
        ],
        out_specs=pl.BlockSpec((tm, c), lambda b, be, nt: (b, 0)),
    )
    return pl.pallas_call(
        _experts_kernel,
        grid_spec=grid_spec,
        out_shape=jax.ShapeDtypeStruct((cap, c), U32),
        compiler_params=_params("arbitrary"),
        name="experts",
    )(block_e, n_tiles, xs, wg, wu, wd)


def _combine_kernel(slot_hbm, h_ref, w_ref, mod_ref, g_ref, sg_ref, su_ref, sd_ref, ys_hbm, o_ref,
                    slot_smem, ybuf, sem_idx, sem_rows):
    i = pl.program_id(0)
    t = h_ref.shape[0]
    idx_copy = pltpu.make_async_copy(slot_hbm.at[i], slot_smem, sem_idx)
    idx_copy.start()
    idx_copy.wait()

    def issue(tok, carry):
        for k in range(TOP_K):
            s = slot_smem[k * t + tok]
            pltpu.make_async_copy(ys_hbm.at[pl.ds(s, 1), :], ybuf.at[k, pl.ds(tok, 1), :], sem_rows).start()
        return carry

    lax.fori_loop(0, t, issue, 0)
    h1 = h_ref[...]
    f = _modulate(h1, g_ref[0:1, :], mod_ref[0, 3:4, :], mod_ref[0, 4:5, :]).astype(BF16)
    sg = jnp.dot(f, sg_ref[...], preferred_element_type=F32)
    su = jnp.dot(f, su_ref[...], preferred_element_type=F32)
    y = jnp.dot((sg * _sigmoid(sg) * su).astype(BF16), sd_ref[...], preferred_element_type=F32)
    for k in range(TOP_K):
        pltpu.make_async_copy(ys_hbm.at[pl.ds(0, t), :], ybuf.at[k], sem_rows).wait()
    c = ybuf.shape[2]
    wt = w_ref[...].T
    y_lo = y[:, 0:c]
    y_hi = y[:, c:]
    for k in range(TOP_K):
        lo, hi = _unpack_halves(ybuf[k])
        wk = wt[:, k:k + 1]
        y_lo = y_lo + wk * lo
        y_hi = y_hi + wk * hi
    y = jnp.concatenate([y_lo, y_hi], axis=1)
    o_ref[...] = h1 + mod_ref[0, 5:6, :] * _rms(y, g_ref[1:2, :])


def _combine(slot_tiles, h1, w, mod, g23, sg, su, sd, ys, n_ctx):
    n, d = h1.shape
    t = ROUTE_TILE
    c = ys.shape[1]
    ff = sg.shape[1]
    ctx_tiles = n_ctx // t
    return pl.pallas_call(
        _combine_kernel,
        grid=(n // t,),
        in_specs=[
            pl.BlockSpec(memory_space=pl.ANY),
            pl.BlockSpec((t, d), lambda i: (i, 0)),
            pl.BlockSpec((TOP_K, t), lambda i: (0, i)),
            pl.BlockSpec((1, 6, d), lambda i: (jnp.where(i < ctx_tiles, 1, 0), 0, 0)),
            pl.BlockSpec((2, d), lambda i: (0, 0)),
            pl.BlockSpec((d, ff), lambda i: (0, 0)),
            pl.BlockSpec((d, ff), lambda i: (0, 0)),
            pl.BlockSpec((ff, d), lambda i: (0, 0)),
            pl.BlockSpec(memory_space=pl.ANY),
        ],
        out_specs=pl.BlockSpec((t, d), lambda i: (i, 0)),
        out_shape=jax.ShapeDtypeStruct((n, d), F32),
        scratch_shapes=[
            pltpu.SMEM((TOP_K * t,), I32),
            pltpu.VMEM((TOP_K, t, c), U32),
            pltpu.SemaphoreType.DMA,
            pltpu.SemaphoreType.DMA,
        ],
        compiler_params=_params("arbitrary"),
        name="combine",
    )(slot_tiles, h1, w, mod, g23, sg, su, sd, ys)


def _rope_tables(n_ctx, seq):
    nf = HEAD_DIM // 4
    inv = ROPE_THETA ** (-jnp.arange(nf, dtype=F32) / nf)
    pos = jnp.arange(seq, dtype=I32)
    row_ang = (pos // GRID_W).astype(F32)[:, None] * inv
    col_ang = (pos % GRID_W).astype(F32)[:, None] * inv
    cos = jnp.concatenate([jnp.cos(row_ang)] * 2 + [jnp.cos(col_ang)] * 2, axis=1)
    sin = jnp.concatenate([-jnp.sin(row_ang), jnp.sin(row_ang), -jnp.sin(col_ang), jnp.sin(col_ang)], axis=1)
    cos = jnp.concatenate([jnp.ones((n_ctx, HEAD_DIM), F32), cos], axis=0)
    sin = jnp.concatenate([jnp.zeros((n_ctx, HEAD_DIM), F32), sin], axis=0)
    return cos, sin


def _split_router(router):
    hi = router.astype(BF16)
    lo = (router - hi.astype(F32)).astype(BF16)
    return jnp.concatenate([hi, lo], axis=1)


def _moe(layer, h1, fp, logits, mod, g23, bias, wg, wu, wd, sg, su, sd, n_ctx):
    n = h1.shape[0]
    t = ROUTE_TILE
    tm = EXPERT_TILE
    top_e, w, rank, cnt = _route(logits, bias.reshape(N_EXPERTS, 1))
    counts = cnt[:, 0].astype(I32)
    padded = (counts + tm - 1) // tm * tm
    pad_end = jnp.cumsum(padded)
    pad_start = pad_end - padded
    onehot = top_e[:, :, None] == jnp.arange(N_EXPERTS, dtype=I32)
    slot = rank + jnp.sum(jnp.where(onehot, pad_start, 0), axis=-1)
    slot_tiles = slot.reshape(TOP_K, n // t, t).transpose(1, 0, 2).reshape(n // t, TOP_K * t)
    cap = n * TOP_K + N_EXPERTS * tm
    nb = cap // tm
    n_tiles = (pad_end[-1] // tm).astype(I32).reshape(1)
    tile_row = jnp.minimum(jnp.arange(nb, dtype=I32), n_tiles[0] - 1) * tm
    block_e = jnp.sum((pad_end[None, :] <= tile_row[:, None]).astype(I32), axis=1)
    block_e = jnp.minimum(block_e, N_EXPERTS - 1).astype(I32)
    xs = _dispatch(slot_tiles, fp, jnp.zeros((cap, fp.shape[1]), U32))
    ys = _experts(layer, block_e, n_tiles, xs, wg, wu, wd)
    return _combine(slot_tiles, h1, w, mod, g23, sg, su, sd, ys, n_ctx)


def kernel(x, c, ctx, c_ctx, ada_w, ada_b, norm_g, ev_w_in, ev_w_out, ev_q_norm, ev_k_norm, ev_dw_w, ev_dw_b,
           ev_ln_g, ev_ln_b, od_w_in, od_w_out, od_lambda, od_subln_g, moe_router, moe_bias, moe_w_gate,
           moe_w_up, moe_w_down, sh_w_gate, sh_w_up, sh_w_down):
    b, seq, d = x.shape
    n_ctx = ctx.shape[1]
    assert b == 1 and n_ctx % ROW_TILE == 0 and seq % ROW_TILE == 0 and seq % GRID_W == 0
    depth = ada_w.shape[0]
    assert depth == 2
    cos, sin = _rope_tables(n_ctx, seq)
    cvec = jnp.stack([c[0], c_ctx], axis=1)
    mod_all = _ada(cvec, ada_w, ada_b).reshape(depth, 2, 6, d)
    h = jnp.concatenate([ctx[0], x[0]], axis=0)
    qscale = math.log2(math.e) * HEAD_DIM ** -0.5
    c_conv = ev_dw_w.shape[-1]
    wg, wu, wd = moe_w_gate.astype(BF16), moe_w_up.astype(BF16), moe_w_down.astype(BF16)
    sg, su, sd = sh_w_gate.astype(BF16), sh_w_up.astype(BF16), sh_w_down.astype(BF16)

    mod = mod_all[0]
    gains = norm_g[0]
    nq = A_HEADS * HEAD_DIM
    nkv = 2 * A_KV_HEADS * HEAD_DIM
    w_in = jnp.concatenate([ev_w_in[0][:, :nq], ev_w_in[0][:, nq + nkv:], ev_w_in[0][:, nq:nq + nkv]], axis=1)
    qkvu = _modmm(h, mod, gains[0:1], w_in.astype(BF16), n_ctx, col_tile=w_in.shape[1] // 2)
    u_col0 = nq
    k_col0 = nq + 2 * c_conv
    v_col0 = k_col0 + A_KV_HEADS * HEAD_DIM
    q, qmax = _prep(qkvu, 0, 0, A_HEADS, cos, sin, ev_q_norm[0].reshape(1, HEAD_DIM), True, qscale, False)
    kt, kmax = _prep(qkvu, 0, k_col0, A_KV_HEADS, cos, sin, ev_k_norm[0].reshape(1, HEAD_DIM), True, 1.0, True)
    o_attn = _gqa_attention(q, qmax, kt, kmax, qkvu, v_col0, n_ctx)
    o_conv = _conformer_conv(qkvu, u_col0, c_conv, ev_dw_w[0], ev_dw_b[0].reshape(1, -1),
                             ev_ln_g[0].reshape(1, -1), ev_ln_b[0].reshape(1, -1), n_ctx)
    h1, fp, logits = _post([o_attn, o_conv], ev_w_out[0].astype(BF16), h, 0, mod, gains[1:3],
                           _split_router(moe_router[0]), n_ctx)
    h = _moe(0, h1, fp, logits, mod, gains[2:4], moe_bias[0], wg, wu, wd, sg[0], su[0], sd[0], n_ctx)

    mod = mod_all[1]
    gains = norm_g[1]
    lam_init = 0.8 - 0.6 * math.exp(-0.3 * 1)
    lp = od_lambda[0].astype(F32)
    lam = (jnp.exp(jnp.sum(lp[0] * lp[1])) - jnp.exp(jnp.sum(lp[2] * lp[3])) + lam_init).reshape(1)
    qkv = _modmm(h, mod, gains[0:1], od_w_in[0].astype(BF16), n_ctx, col_tile=od_w_in.shape[2] // 4)
    n_qk = 2 * DIFF_HEADS
    ones = jnp.ones((1, HEAD_DIM), F32)
    q, qmax = _prep(qkv, n_ctx, 0, n_qk, cos, sin, ones, False, qscale, False)
    kt, kmax = _prep(qkv, 0, n_qk * HEAD_DIM, n_qk, cos, sin, ones, False, 1.0, True)
    o_attn = _diff_attention(lam, q, qmax, kt, kmax, qkv, 2 * n_qk * HEAD_DIM, od_subln_g[0].reshape(1, -1),
                             1.0 - lam_init, n_ctx)
    h1, fp, logits = _post([o_attn], od_w_out[0].astype(BF16), h, n_ctx, mod, gains[1:3],
                           _split_router(moe_router[1]), n_ctx)
    h = _moe(1, h1, fp, logits, mod, gains[2:4], moe_bias[1], wg, wu, wd, sg[1], su[1], sd[1], 0)
    return h[None]
```

```python
import functools
import math

import jax
import jax.numpy as jnp
from jax import lax
from jax.experimental import pallas as pl
from jax.experimental.pallas import tpu as pltpu

F32 = jnp.float32
BF16 = jnp.bfloat16
U32 = jnp.uint32
I32 = jnp.int32

GRID_W = 64
HEAD_DIM = 128
ROPE_THETA = 10000.0
EPS = 1e-6
A_HEADS = 8
A_KV_HEADS = 2
CONV_W = 31
DIFF_HEADS = 8
N_EXPERTS = 64
TOP_K = 8
N_GROUPS = 8
TOPK_GROUPS = 4
ROUTED_SCALE = 2.5

LANES = 128
SUBLANES = 8
VMEM_LIMIT_BYTES = 56 * 1024 * 1024

ROW_TILE = 256
KV_CHUNK = 256
LAT_SUB = 8
Q_TILE_GQA = 128
Q_TILE_DIFF = 512
ROUTE_TILE = 256
EXPERT_TILE = 256
CONV_HALO = 16
NEG_BIG = -1e30
FAST_BOUND_MAX = 60.0


def _params(*sem):
    return pltpu.CompilerParams(dimension_semantics=sem, vmem_limit_bytes=VMEM_LIMIT_BYTES)


def _sigmoid(x):
    return 1.0 / (1.0 + jnp.exp(-x))


def _rms(x, g):
    return x * lax.rsqrt(jnp.mean(x * x, axis=-1, keepdims=True) + EPS) * g


def _modulate(x, g, shift, scale):
    return _rms(x, g) * (1.0 + scale) + shift


def _pack_halves(x):
    c = x.shape[1] // 2
    lo = lax.bitcast_convert_type(x[:, :c].astype(BF16).astype(F32), U32)
    hi = lax.bitcast_convert_type(x[:, c:].astype(BF16).astype(F32), U32)
    return (hi & jnp.uint32(0xFFFF0000)) | (lo >> 16)


def _unpack_halves(w):
    lo = lax.bitcast_convert_type(w << 16, F32)
    hi = lax.bitcast_convert_type(w & jnp.uint32(0xFFFF0000), F32)
    return lo, hi


ADA_COLS = 512
ADA_KCHUNK = 64


def _ada_kernel(cv_ref, w_ref, b_ref, o_ref, s_ref):
    cv = cv_ref[...]
    s_ref[...] = cv * _sigmoid(cv)
    d = cv.shape[0]
    tn = o_ref.shape[-1]

    def body(k, acc):
        a0, a1 = acc
        r0 = pl.multiple_of(k * ADA_KCHUNK, ADA_KCHUNK)
        w = w_ref[0, pl.ds(r0, ADA_KCHUNK), :]
        s = s_ref[pl.ds(r0, ADA_KCHUNK), :]
        p0 = (w * s[:, 0:1]).reshape(ADA_KCHUNK // SUBLANES, SUBLANES, tn)
        p1 = (w * s[:, 1:2]).reshape(ADA_KCHUNK // SUBLANES, SUBLANES, tn)
        return a0 + jnp.sum(p0, axis=0), a1 + jnp.sum(p1, axis=0)

    z = jnp.zeros((SUBLANES, tn), F32)
    a0, a1 = lax.fori_loop(0, d // ADA_KCHUNK, body, (z, z))
    b = b_ref[0]
    o_ref[0, 0:1, :] = jnp.sum(a0, axis=0, keepdims=True) + b
    o_ref[0, 1:2, :] = jnp.sum(a1, axis=0, keepdims=True) + b


def _ada(cvec, ada_w, ada_b):
    depth, d, n = ada_w.shape
    return pl.pallas_call(
        _ada_kernel,
        grid=(depth, n // ADA_COLS),
        in_specs=[
            pl.BlockSpec((d, 2), lambda l, j: (0, 0)),
            pl.BlockSpec((1, d, ADA_COLS), lambda l, j: (l, 0, j)),
            pl.BlockSpec((1, 1, ADA_COLS), lambda l, j: (l, 0, j)),
        ],
        out_specs=pl.BlockSpec((1, 2, ADA_COLS), lambda l, j: (l, 0, j)),
        out_shape=jax.ShapeDtypeStruct((depth, 2, n), F32),
        scratch_shapes=[pltpu.VMEM((d, 2), F32)],
        compiler_params=_params("arbitrary", "arbitrary"),
        name="ada",
    )(cvec, ada_w, ada_b.reshape(depth, 1, n))


def _modmm_kernel(h_ref, mod_ref, g_ref, w_ref, o_ref):
    a = _modulate(h_ref[...], g_ref[...], mod_ref[0, 0:1, :], mod_ref[0, 1:2, :])
    o_ref[...] = jnp.dot(a.astype(BF16), w_ref[...], preferred_element_type=F32).astype(o_ref.dtype)


def _modmm(h, mod, g, w, n_ctx, col_tile):
    r, d = h.shape
    n = w.shape[1]
    ctx_tiles = n_ctx // ROW_TILE
    return pl.pallas_call(
        _modmm_kernel,
        grid=(n // col_tile, r // ROW_TILE),
        in_specs=[
            pl.BlockSpec((ROW_TILE, d), lambda j, i: (i, 0)),
            pl.BlockSpec((1, 6, d), lambda j, i: (jnp.where(i < ctx_tiles, 1, 0), 0, 0)),
            pl.BlockSpec((1, d), lambda j, i: (0, 0)),
            pl.BlockSpec((d, col_tile), lambda j, i: (0, j)),
        ],
        out_specs=pl.BlockSpec((ROW_TILE, col_tile), lambda j, i: (i, j)),
        out_shape=jax.ShapeDtypeStruct((r, n), BF16),
        compiler_params=_params("arbitrary", "arbitrary"),
        name="modmm",
    )(h, mod, g, w)


def _rope(x, c, s):
    lane = lax.broadcasted_iota(I32, x.shape, 1)
    first = (lane % 64) < 32
    partner = jnp.where(first, pltpu.roll(x, 96, axis=1), pltpu.roll(x, 32, axis=1))
    return x * c + partner * s


def _prep_kernel(x_ref, c_ref, s_ref, g_ref, o_ref, st_ref, *, n_heads, norm, scale, transpose):
    c = c_ref[...]
    s = s_ref[...]
    stats = []
    for h in range(n_heads):
        x = x_ref[:, h * HEAD_DIM:(h + 1) * HEAD_DIM].astype(F32)
        if norm:
            x = _rms(x, g_ref[...])
        y = (_rope(x, c, s) * scale).astype(BF16).astype(F32)
        sq = jnp.max(jnp.sum(y * y, axis=-1, keepdims=True), axis=0, keepdims=True)
        stats.append(jnp.broadcast_to(sq, (1, LANES)))
        if transpose:
            o_ref[h, 0] = y.T.astype(o_ref.dtype)
        else:
            o_ref[:, h * HEAD_DIM:(h + 1) * HEAD_DIM] = y.astype(o_ref.dtype)
    st_ref[0] = jnp.concatenate(stats, axis=0)


def _prep(x, row0, col0, n_heads, cos, sin, g, norm, scale, transpose):
    r = x.shape[0] - row0
    width = n_heads * HEAD_DIM
    cb = col0 // width
    tr = KV_CHUNK
    nt = r // tr
    rb = row0 // tr
    if transpose:
        o_spec = pl.BlockSpec((n_heads, 1, HEAD_DIM, tr), lambda i: (0, i, 0, 0))
        o_shape = jax.ShapeDtypeStruct((n_heads, nt, HEAD_DIM, tr), BF16)
    else:
        o_spec = pl.BlockSpec((tr, width), lambda i: (i, 0))
        o_shape = jax.ShapeDtypeStruct((r, width), BF16)
    out, stats = pl.pallas_call(
        functools.partial(_prep_kernel, n_heads=n_heads, norm=norm, scale=scale, transpose=transpose),
        grid=(nt,),
        in_specs=[
            pl.BlockSpec((tr, width), lambda i: (i + rb, cb)),
            pl.BlockSpec((tr, HEAD_DIM), lambda i: (i + rb, 0)),
            pl.BlockSpec((tr, HEAD_DIM), lambda i: (i + rb, 0)),
            pl.BlockSpec((1, HEAD_DIM), lambda i: (0, 0)),
        ],
        out_specs=[o_spec, pl.BlockSpec((1, n_heads, LANES), lambda i: (i, 0, 0))],
        out_shape=[o_shape, jax.ShapeDtypeStruct((nt, n_heads, LANES), F32)],
        compiler_params=_params("arbitrary"),
        name="prep_k" if transpose else "prep_q",
    )(x, cos, sin, g)
    return out, jnp.sqrt(jnp.max(stats[:, :, 0], axis=0))


def _lane_tile(x, n):
    return x if n == 1 else jnp.concatenate([x] * n, axis=1)


def _lane_fold(p):
    out = p[:, 0:LANES]
    for j in range(1, p.shape[1] // LANES):
        out = out + p[:, j * LANES:(j + 1) * LANES]
    return out


def _softmax_step(q, kt, v, m_ref, l_ref, acc_ref, online):
    s = jnp.dot(q, kt, preferred_element_type=F32)
    tiles = s.shape[1] // LANES
    if online:
        m_prev = m_ref[...]
        m_new = jnp.maximum(m_prev, jnp.max(s, axis=-1, keepdims=True))
        alpha = jnp.exp2(m_prev - m_new)
        p = jnp.exp2(s - _lane_tile(m_new, tiles))
        l_ref[...] = alpha * l_ref[...] + _lane_fold(p)
        acc_ref[...] = (_lane_tile(alpha, acc_ref.shape[1] // LANES) * acc_ref[...]
                        + jnp.dot(p.astype(BF16), v, preferred_element_type=F32))
        m_ref[...] = m_new
    else:
        p = jnp.exp2(s - _lane_tile(m_ref[...], tiles))
        l_ref[...] += _lane_fold(p)
        acc_ref[...] += jnp.dot(p.astype(BF16), v, preferred_element_type=F32)


def _scan_keys(kt_ref, n_k, v_ref, n_ctx, n_lat, see_latent, step):
    for cc in range(n_ctx // KV_CHUNK):
        step([kt_ref[j, cc] for j in range(n_k)], v_ref[cc * KV_CHUNK:(cc + 1) * KV_CHUNK, :])
    ctx_chunks = n_ctx // KV_CHUNK

    def latent():
        def body(c, carry):
            base = ctx_chunks + c * LAT_SUB
            kts = [jnp.concatenate([kt_ref[j, base + u] for u in range(LAT_SUB)], axis=1) for j in range(n_k)]
            r0 = pl.multiple_of(n_ctx + c * (LAT_SUB * KV_CHUNK), KV_CHUNK)
            step(kts, v_ref[pl.ds(r0, LAT_SUB * KV_CHUNK), :])
            return carry

        lax.fori_loop(0, n_lat, body, 0)

    if see_latent is None:
        latent()
    else:
        pl.when(see_latent)(latent)


def _row_bound(q, kmax):
    qf = q.astype(F32)
    return jnp.sqrt(jnp.sum(qf * qf, axis=-1, keepdims=True)) * kmax


def _gqa_kernel(fast_ref, kmax_ref, q_ref, kt_ref, v_ref, o_ref, qs_ref, m_ref, l_ref, acc_ref,
                *, ctx_tiles, n_ctx, n_lat):
    g = pl.program_id(0)
    tq = q_ref.shape[0]
    group = q_ref.shape[1] // HEAD_DIM
    for u in range(group):
        qs_ref[u * tq:(u + 1) * tq, :] = q_ref[:, u * HEAD_DIM:(u + 1) * HEAD_DIM]
    l_ref[...] = jnp.zeros(l_ref.shape, F32)
    acc_ref[...] = jnp.zeros(acc_ref.shape, F32)
    see_latent = pl.program_id(1) >= ctx_tiles
    fast = fast_ref[g] == 1

    def scan(online):
        def step(kts, v):
            _softmax_step(qs_ref[...], kts[0], v, m_ref, l_ref, acc_ref, online)

        _scan_keys(kt_ref, 1, v_ref, n_ctx, n_lat, see_latent, step)

    @pl.when(fast)
    def _():
        m_ref[...] = jnp.broadcast_to(_row_bound(qs_ref[...], kmax_ref[g]), m_ref.shape)
        scan(False)

    @pl.when(jnp.logical_not(fast))
    def _():
        m_ref[...] = jnp.full(m_ref.shape, NEG_BIG, F32)
        scan(True)

    o = acc_ref[...] / jnp.sum(l_ref[...], axis=-1, keepdims=True)
    for u in range(group):
        o_ref[:, u * HEAD_DIM:(u + 1) * HEAD_DIM] = o[u * tq:(u + 1) * tq, :].astype(o_ref.dtype)


def _kv_chunks(r, n_ctx):
    n_lat_chunks = (r - n_ctx) // KV_CHUNK
    assert n_ctx % KV_CHUNK == 0 and n_lat_chunks % LAT_SUB == 0
    return n_lat_chunks // LAT_SUB


def _gqa_attention(q, qmax, kt, kmax, qkvu, v_col0, n_ctx):
    r = q.shape[0]
    group = A_HEADS // A_KV_HEADS
    tq = Q_TILE_GQA
    nc = r // KV_CHUNK
    vb = v_col0 // HEAD_DIM
    fast = (jnp.max(qmax.reshape(A_KV_HEADS, group), axis=1) * kmax <= FAST_BOUND_MAX).astype(I32)
    kern = functools.partial(_gqa_kernel, ctx_tiles=n_ctx // tq, n_ctx=n_ctx, n_lat=_kv_chunks(r, n_ctx))
    smem = pl.BlockSpec(memory_space=pltpu.SMEM)
    return pl.pallas_call(
        kern,
        grid=(A_KV_HEADS, r // tq),
        in_specs=[
            smem, smem,
            pl.BlockSpec((tq, group * HEAD_DIM), lambda g, i: (i, g)),
            pl.BlockSpec((1, nc, HEAD_DIM, KV_CHUNK), lambda g, i: (g, 0, 0, 0)),
            pl.BlockSpec((r, HEAD_DIM), lambda g, i: (0, vb + g)),
        ],
        out_specs=pl.BlockSpec((tq, group * HEAD_DIM), lambda g, i: (i, g)),
        out_shape=jax.ShapeDtypeStruct((r, A_HEADS * HEAD_DIM), BF16),
        scratch_shapes=[
            pltpu.VMEM((group * tq, HEAD_DIM), BF16),
            pltpu.VMEM((group * tq, LANES), F32),
            pltpu.VMEM((group * tq, LANES), F32),
            pltpu.VMEM((group * tq, HEAD_DIM), F32),
        ],
        compiler_params=_params("arbitrary", "arbitrary"),
        name="gqa_attn",
    )(fast, kmax, q, kt, qkvu)


def _diff_kernel(lam_ref, fast_ref, kmax_ref, q_ref, kt_ref, v_ref, g_ref, o_ref, m0, l0, a0, m1, l1, a1,
                 *, out_scale, n_ctx, n_lat):
    h = pl.program_id(0)
    comps = ((m0, l0, a0), (m1, l1, a1))
    for _, l_ref, a_ref in comps:
        l_ref[...] = jnp.zeros(l_ref.shape, F32)
        a_ref[...] = jnp.zeros(a_ref.shape, F32)
    fast = fast_ref[h] == 1

    def q_comp(c):
        return q_ref[:, c * HEAD_DIM:(c + 1) * HEAD_DIM]

    def scan(online):
        def step(kts, v):
            for c, (m_ref, l_ref, a_ref) in enumerate(comps):
                _softmax_step(q_comp(c), kts[c], v, m_ref, l_ref, a_ref, online)

        _scan_keys(kt_ref, 2, v_ref, n_ctx, n_lat, None, step)

    @pl.when(fast)
    def _():
        for c, (m_ref, _, _) in enumerate(comps):
            m_ref[...] = jnp.broadcast_to(_row_bound(q_comp(c), kmax_ref[2 * h + c]), m_ref.shape)
        scan(False)

    @pl.when(jnp.logical_not(fast))
    def _():
        for m_ref, _, _ in comps:
            m_ref[...] = jnp.full(m_ref.shape, NEG_BIG, F32)
        scan(True)

    o = (a0[...] / jnp.sum(l0[...], axis=-1, keepdims=True)
         - lam_ref[0] * (a1[...] / jnp.sum(l1[...], axis=-1, keepdims=True)))
    o_ref[...] = (_rms(o, g_ref[...]) * out_scale).astype(o_ref.dtype)


def _diff_attention(lam, q, qmax, kt, kmax, qkv, v_col0, subln_g, out_scale, n_ctx):
    r = qkv.shape[0]
    seq = q.shape[0]
    tq = math.gcd(Q_TILE_DIFF, seq)
    nc = r // KV_CHUNK
    dv = 2 * HEAD_DIM
    vb = v_col0 // dv
    fast = (jnp.max((qmax * kmax).reshape(DIFF_HEADS, 2), axis=1) <= FAST_BOUND_MAX).astype(I32)
    kern = functools.partial(_diff_kernel, out_scale=out_scale, n_ctx=n_ctx, n_lat=_kv_chunks(r, n_ctx))
    stat = pltpu.VMEM((tq, LANES), F32)
    acc = pltpu.VMEM((tq, dv), F32)
    smem = pl.BlockSpec(memory_space=pltpu.SMEM)
    return pl.pallas_call(
        kern,
        grid=(DIFF_HEADS, seq // tq),
        in_specs=[
            smem, smem, smem,
            pl.BlockSpec((tq, dv), lambda h, i: (i, h)),
            pl.BlockSpec((2, nc, HEAD_DIM, KV_CHUNK), lambda h, i: (h, 0, 0, 0), pipeline_mode=pl.Buffered(1)),
            pl.BlockSpec((r, dv), lambda h, i: (0, vb + h), pipeline_mode=pl.Buffered(1)),
            pl.BlockSpec((1, dv), lambda h, i: (0, 0)),
        ],
        out_specs=pl.BlockSpec((tq, dv), lambda h, i: (i, h)),
        out_shape=jax.ShapeDtypeStruct((seq, DIFF_HEADS * dv), BF16),
        scratch_shapes=[stat, stat, acc, stat, stat, acc],
        compiler_params=_params("arbitrary", "arbitrary"),
        name="diff_attn",
    )(lam, fast, kmax, q, kt, qkv, subln_g)


def _conv_kernel(ap_ref, gp_ref, ac_ref, gc_ref, an_ref, gn_ref, w_ref, b_ref, lg_ref, lb_ref, o_ref, h_ref, sh_ref,
                 *, seq_starts, seq_ends):
    i = pl.program_id(0)
    tr = ac_ref.shape[0]
    is_start = functools.reduce(jnp.logical_or, [i == s for s in seq_starts])
    is_end = functools.reduce(jnp.logical_or, [i == e for e in seq_ends])

    def glu(a_ref, g_ref):
        return a_ref[...].astype(F32) * _sigmoid(g_ref[...].astype(F32))

    h_ref[0:CONV_HALO, :] = jnp.where(is_start, 0.0, glu(ap_ref, gp_ref))
    h_ref[CONV_HALO:CONV_HALO + tr, :] = glu(ac_ref, gc_ref)
    h_ref[CONV_HALO + tr:, :] = jnp.where(is_end, 0.0, glu(an_ref, gn_ref))
    rows = sh_ref.shape[1]
    for b in range(1, SUBLANES):
        sh_ref[b - 1] = h_ref[b:b + rows, :]
    acc = jnp.zeros((tr, ac_ref.shape[1]), F32)
    base = CONV_HALO - CONV_W // 2
    for j in range(CONV_W):
        a, b = divmod(base + j, SUBLANES)
        r0 = a * SUBLANES
        src = h_ref[r0:r0 + tr, :] if b == 0 else sh_ref[b - 1, r0:r0 + tr, :]
        acc = acc + w_ref[j:j + 1, :] * src
    y = acc + b_ref[...]
    mu = jnp.mean(y, axis=-1, keepdims=True)
    var = jnp.mean(jnp.square(y - mu), axis=-1, keepdims=True)
    z = (y - mu) * lax.rsqrt(var + EPS) * lg_ref[...] + lb_ref[...]
    o_ref[...] = (z * _sigmoid(z)).astype(o_ref.dtype)


CONV_TILE = 128


def _conformer_conv(qkvu, u_col0, c_conv, dw_w, dw_b, ln_g, ln_b, n_ctx):
    r = qkvu.shape[0]
    tr = CONV_TILE
    nt = r // tr
    hb = tr // CONV_HALO
    ab = u_col0 // c_conv
    gb = ab + 1
    last_halo = r // CONV_HALO - 1
    seq_starts = (0, n_ctx // tr)
    seq_ends = (n_ctx // tr - 1, nt - 1)

    def prev(col):
        return pl.BlockSpec((CONV_HALO, c_conv), lambda i: (jnp.maximum(i * hb - 1, 0), col))

    def cur(col):
        return pl.BlockSpec((tr, c_conv), lambda i: (i, col))

    def nxt(col):
        return pl.BlockSpec((CONV_HALO, c_conv), lambda i: (jnp.minimum((i + 1) * hb, last_halo), col))

    vec = pl.BlockSpec((1, c_conv), lambda i: (0, 0))
    return pl.pallas_call(
        functools.partial(_conv_kernel, seq_starts=seq_starts, seq_ends=seq_ends),
        grid=(nt,),
        in_specs=[prev(ab), prev(gb), cur(ab), cur(gb), nxt(ab), nxt(gb),
                  pl.BlockSpec((CONV_W, c_conv), lambda i: (0, 0)), vec, vec, vec],
        out_specs=pl.BlockSpec((tr, c_conv), lambda i: (i, 0)),
        out_shape=jax.ShapeDtypeStruct((r, c_conv), BF16),
        scratch_shapes=[pltpu.VMEM((tr + 2 * CONV_HALO, c_conv), F32),
                        pltpu.VMEM((SUBLANES - 1, tr + 2 * CONV_HALO - SUBLANES, c_conv), F32)],
        compiler_params=_params("arbitrary"),
        name="conformer_conv",
    )(qkvu, qkvu, qkvu, qkvu, qkvu, qkvu, dw_w, dw_b, ln_g, ln_b)


def _post_kernel(*refs, n_in):
    o_refs = refs[:n_in]
    w_ref, h_ref, mod_ref, g_ref, rr_ref, h1_ref, fp_ref, lg_ref = refs[n_in:]
    k0 = 0
    out = None
    for o_in in o_refs:
        kw = o_in.shape[1]
        part = jnp.dot(o_in[...], w_ref[k0:k0 + kw, :], preferred_element_type=F32)
        out = part if out is None else out + part
        k0 += kw
    h1 = h_ref[...] + mod_ref[0, 2:3, :] * _rms(out, g_ref[0:1, :])
    h1_ref[...] = h1
    f = _modulate(h1, g_ref[1:2, :], mod_ref[0, 3:4, :], mod_ref[0, 4:5, :])
    fp_ref[...] = _pack_halves(f)
    f_hi = f.astype(BF16)
    f_lo = (f - f_hi.astype(F32)).astype(BF16)
    l2 = (jnp.dot(f_hi, rr_ref[...], preferred_element_type=F32)
          + jnp.dot(f_lo, rr_ref[...], preferred_element_type=F32))
    lg_ref[...] = l2 + pltpu.roll(l2, N_EXPERTS, axis=1)


def _post(o_parts, w_out, h, row_off, mod, g12, rr, n_ctx):
    n = o_parts[0].shape[0]
    d = h.shape[1]
    tm = ROW_TILE
    off = row_off // tm
    ctx_tiles = (n_ctx - row_off) // tm
    in_specs = [pl.BlockSpec((tm, o.shape[1]), lambda i: (i, 0)) for o in o_parts]
    in_specs += [
        pl.BlockSpec((d, d), lambda i: (0, 0)),
        pl.BlockSpec((tm, d), lambda i: (i + off, 0)),
        pl.BlockSpec((1, 6, d), lambda i: (jnp.where(i < ctx_tiles, 1, 0), 0, 0)),
        pl.BlockSpec((2, d), lambda i: (0, 0)),
        pl.BlockSpec((d, 2 * N_EXPERTS), lambda i: (0, 0)),
    ]
    return pl.pallas_call(
        functools.partial(_post_kernel, n_in=len(o_parts)),
        grid=(n // tm,),
        in_specs=in_specs,
        out_specs=[
            pl.BlockSpec((tm, d), lambda i: (i, 0)),
            pl.BlockSpec((tm, d // 2), lambda i: (i, 0)),
            pl.BlockSpec((tm, 2 * N_EXPERTS), lambda i: (i, 0)),
        ],
        out_shape=[
            jax.ShapeDtypeStruct((n, d), F32),
            jax.ShapeDtypeStruct((n, d // 2), U32),
            jax.ShapeDtypeStruct((n, 2 * N_EXPERTS), F32),
        ],
        compiler_params=_params("arbitrary"),
        name="post_mixer",
    )(*o_parts, w_out, h, mod, g12, rr)


def _route_kernel(lg_ref, bias_ref, e_ref, w_ref, rank_ref, cnt_ref):
    t = lg_ref.shape[0]
    gsz = N_EXPERTS // N_GROUPS

    @pl.when(pl.program_id(0) == 0)
    def _():
        cnt_ref[...] = jnp.zeros(cnt_ref.shape, F32)

    logits = lg_ref[...].T[0:N_EXPERTS, :]
    scores = _sigmoid(logits)
    choice = scores + bias_ref[...]
    sub = lax.broadcasted_iota(I32, (gsz, t), 0).astype(F32)
    grp_rows = []
    for g in range(N_GROUPS):
        cg = choice[g * gsz:(g + 1) * gsz, :]
        m1 = jnp.max(cg, axis=0, keepdims=True)
        i1 = jnp.min(jnp.where(cg == m1, sub, float(gsz)), axis=0, keepdims=True)
        m2 = jnp.max(jnp.where(sub == i1, -jnp.inf, cg), axis=0, keepdims=True)
        grp_rows.append(m1 + m2)
    grp = jnp.concatenate(grp_rows, axis=0)
    gidx = lax.broadcasted_iota(I32, (N_GROUPS, t), 0).astype(F32)
    gsel = jnp.zeros((N_GROUPS, t), F32)
    for _ in range(TOPK_GROUPS):
        gm = jnp.max(grp, axis=0, keepdims=True)
        gi = jnp.min(jnp.where(grp == gm, gidx, float(N_GROUPS)), axis=0, keepdims=True)
        hit = gidx == gi
        gsel = jnp.where(hit, 1.0, gsel)
        grp = jnp.where(hit, -jnp.inf, grp)
    eidx = lax.broadcasted_iota(I32, (N_EXPERTS, t), 0).astype(F32)
    mc_rows = []
    for g in range(N_GROUPS):
        mc_rows.append(jnp.where(gsel[g:g + 1, :] > 0.5, choice[g * gsz:(g + 1) * gsz, :], -jnp.inf))
    mc = jnp.concatenate(mc_rows, axis=0)
    sel_all = jnp.zeros((N_EXPERTS, t), F32)
    e_rows, w_rows = [], []
    for _ in range(TOP_K):
        mx = jnp.max(mc, axis=0, keepdims=True)
        ei = jnp.min(jnp.where(mc == mx, eidx, float(N_EXPERTS)), axis=0, keepdims=True)
        hit = eidx == ei
        e_rows.append(ei)
        w_rows.append(jnp.sum(jnp.where(hit, scores, 0.0), axis=0, keepdims=True))
        sel_all = jnp.where(hit, 1.0, sel_all)
        mc = jnp.where(hit, -jnp.inf, mc)
    e_all = jnp.concatenate(e_rows, axis=0)
    w_all = jnp.concatenate(w_rows, axis=0)
    w_ref[...] = w_all / jnp.sum(w_all, axis=0, keepdims=True) * ROUTED_SCALE
    e_ref[...] = e_all.astype(I32)
    src = lax.broadcasted_iota(I32, (t, t), 0)
    dst = lax.broadcasted_iota(I32, (t, t), 1)
    upper = jnp.where(src < dst, 1.0, 0.0).astype(BF16)
    pos = jnp.dot(sel_all.astype(BF16), upper, preferred_element_type=F32) + cnt_ref[:, 0:1]
    r_rows = [jnp.sum(jnp.where(eidx == e_rows[k], pos, 0.0), axis=0, keepdims=True) for k in range(TOP_K)]
    rank_ref[...] = jnp.concatenate(r_rows, axis=0).astype(I32)
    cnt_ref[...] = cnt_ref[...] + jnp.sum(sel_all, axis=1, keepdims=True)


def _route(logits, bias):
    n = logits.shape[0]
    t = ROUTE_TILE
    row = pl.BlockSpec((TOP_K, t), lambda i: (0, i))
    return pl.pallas_call(
        _route_kernel,
        grid=(n // t,),
        in_specs=[
            pl.BlockSpec((t, 2 * N_EXPERTS), lambda i: (i, 0)),
            pl.BlockSpec((N_EXPERTS, 1), lambda i: (0, 0)),
        ],
        out_specs=[row, row, row, pl.BlockSpec((N_EXPERTS, LANES), lambda i: (0, 0))],
        out_shape=[
            jax.ShapeDtypeStruct((TOP_K, n), I32),
            jax.ShapeDtypeStruct((TOP_K, n), F32),
            jax.ShapeDtypeStruct((TOP_K, n), I32),
            jax.ShapeDtypeStruct((N_EXPERTS, LANES), F32),
        ],
        compiler_params=_params("arbitrary"),
        name="route",
    )(logits, bias)


def _dispatch_kernel(meta_ref, nt_ref, slot_hbm, fp_ref, xs_hbm, slot_smem, zero_ref, sem_idx, sem_rows, sem_zero):
    i = pl.program_id(0)
    t = fp_ref.shape[0]
    idx_copy = pltpu.make_async_copy(slot_hbm.at[i], slot_smem, sem_idx)
    idx_copy.start()
    idx_copy.wait()

    def issue(tok, carry):
        for k in range(TOP_K):
            s = slot_smem[k * t + tok]
            pltpu.make_async_copy(fp_ref.at[pl.ds(tok, 1), :], xs_hbm.at[pl.ds(s, 1), :],
                                  sem_rows).start(priority=k % 2)
        return carry

    lax.fori_loop(0, t, issue, 0)
    for _ in range(TOP_K):
        pltpu.make_async_copy(fp_ref, xs_hbm.at[pl.ds(0, t), :], sem_rows).wait()

    @pl.when(i == pl.num_programs(0) - 1)
    def _():
        tm = zero_ref.shape[0]
        n_all = xs_hbm.shape[0] // tm
        zero_ref[...] = jnp.zeros(zero_ref.shape, zero_ref.dtype)

        def pad_row(row):
            return pltpu.make_async_copy(zero_ref.at[pl.ds(0, 1), :], xs_hbm.at[pl.ds(row, 1), :], sem_zero)

        def tail_tile(b):
            return pltpu.make_async_copy(zero_ref, xs_hbm.at[pl.ds(pl.multiple_of(b * tm, tm), tm), :], sem_zero)

        def per_expert(e, total):
            first = meta_ref[0, e] + meta_ref[1, e]
            n_pad = meta_ref[2, e] - meta_ref[1, e]

            def one(r, carry):
                pad_row(first + r).start()
                return carry

            lax.fori_loop(0, n_pad, one, 0)
            return total + n_pad

        total = lax.fori_loop(0, N_EXPERTS, per_expert, 0)

        def start_tail(b, carry):
            tail_tile(b).start()
            return carry

        lax.fori_loop(nt_ref[0], n_all, start_tail, 0)

        def wait_row(r, carry):
            pad_row(0).wait()
            return carry

        lax.fori_loop(0, total, wait_row, 0)

        def wait_tail(b, carry):
            tail_tile(b).wait()
            return carry

        lax.fori_loop(nt_ref[0], n_all, wait_tail, 0)


def _dispatch(meta, n_tiles, slot_tiles, fp, cap):
    n, c = fp.shape
    t = ROUTE_TILE
    smem = pl.BlockSpec(memory_space=pltpu.SMEM)
    return pl.pallas_call(
        _dispatch_kernel,
        grid=(n // t,),
        in_specs=[
            smem, smem,
            pl.BlockSpec(memory_space=pl.ANY),
            pl.BlockSpec((t, c), lambda i: (i, 0)),
        ],
        out_specs=pl.BlockSpec(memory_space=pl.ANY),
        out_shape=jax.ShapeDtypeStruct((cap, c), U32),
        scratch_shapes=[
            pltpu.SMEM((TOP_K * t,), I32),
            pltpu.VMEM((EXPERT_TILE, c), U32),
            pltpu.SemaphoreType.DMA, pltpu.SemaphoreType.DMA, pltpu.SemaphoreType.DMA,
        ],
        compiler_params=_params("arbitrary"),
        name="dispatch",
    )(meta, n_tiles, slot_tiles, fp)


def _experts_kernel(be_ref, nt_ref, xs_ref, wg_ref, wu_ref, wd_ref, ys_ref):
    del be_ref
    used = pl.program_id(0) < nt_ref[0]

    @pl.when(jnp.logical_not(used))
    def _():
        ys_ref[...] = jnp.zeros(ys_ref.shape, ys_ref.dtype)

    @pl.when(used)
    def _():
        lo, hi = _unpack_halves(xs_ref[...])
        lo = lo.astype(BF16)
        hi = hi.astype(BF16)
        c = lo.shape[1]
        def proj(w_ref):
            return (jnp.dot(lo, w_ref[0, 0, 0:c, :].astype(BF16), preferred_element_type=F32)
                    + jnp.dot(hi, w_ref[0, 0, c:, :].astype(BF16), preferred_element_type=F32))

        g = proj(wg_ref)
        u = proj(wu_ref)
        a = (g * _sigmoid(g) * u).astype(BF16)
        ys_ref[...] = _pack_halves(jnp.dot(a, wd_ref[0, 0].astype(BF16), preferred_element_type=F32))


def _experts(layer, block_e, n_tiles, xs, wg, wu, wd):
    cap, c = xs.shape
    _, e, d, ff = wg.shape
    tm = EXPERT_TILE

    def row_map(b, be, nt):
        return (jnp.minimum(b, nt[0] - 1), 0)

    grid_spec = pltpu.PrefetchScalarGridSpec(
        num_scalar_prefetch=2,
        grid=(cap // tm,),
        in_specs=[
            pl.BlockSpec((tm, c), row_map),
            pl.BlockSpec((1, 1, d, ff), lambda b, be, nt: (layer, be[b], 0, 0)),
            pl.BlockSpec((1, 1, d, ff), lambda b, be, nt: (layer, be[b], 0, 0)),
            pl.BlockSpec((1, 1, ff, d), lambda b, be, nt: (layer, be[b], 0, 0)),
        ],
        out_specs=pl.BlockSpec((tm, c), lambda b, be, nt: (b, 0)),
    )
    return pl.pallas_call(
        _experts_kernel,
        grid_spec=grid_spec,
        out_shape=jax.ShapeDtypeStruct((cap, c), U32),
        compiler_params=_params("arbitrary"),
        name="experts",
    )(block_e, n_tiles, xs, wg, wu, wd)


def _combine_kernel(slot_hbm, h_ref, w_ref, mod_ref, g_ref, sg_ref, su_ref, sd_ref, ys_hbm, o_ref,
                    slot_smem, ybuf, sem_idx, sem_rows):
    i = pl.program_id(0)
    t = h_ref.shape[0]
    idx_copy = pltpu.make_async_copy(slot_hbm.at[i], slot_smem, sem_idx)
    idx_copy.start()
    idx_copy.wait()

    def issue(tok, carry):
        for k in range(TOP_K):
            s = slot_smem[k * t + tok]
            pltpu.make_async_copy(ys_hbm.at[pl.ds(s, 1), :], ybuf.at[k, pl.ds(tok, 1), :],
                                  sem_rows).start(priority=k % 2)
        return carry

    lax.fori_loop(0, t, issue, 0)
    h1 = h_ref[...]
    f = _modulate(h1, g_ref[0:1, :], mod_ref[0, 3:4, :], mod_ref[0, 4:5, :]).astype(BF16)
    sg = jnp.dot(f, sg_ref[...], preferred_element_type=F32)
    su = jnp.dot(f, su_ref[...], preferred_element_type=F32)
    y = jnp.dot((sg * _sigmoid(sg) * su).astype(BF16), sd_ref[...], preferred_element_type=F32)
    for k in range(TOP_K):
        pltpu.make_async_copy(ys_hbm.at[pl.ds(0, t), :], ybuf.at[k], sem_rows).wait()
    c = ybuf.shape[2]
    wt = w_ref[...].T
    y_lo = y[:, 0:c]
    y_hi = y[:, c:]
    for k in range(TOP_K):
        lo, hi = _unpack_halves(ybuf[k])
        wk = wt[:, k:k + 1]
        y_lo = y_lo + wk * lo
        y_hi = y_hi + wk * hi
    y = jnp.concatenate([y_lo, y_hi], axis=1)
    o_ref[...] = h1 + mod_ref[0, 5:6, :] * _rms(y, g_ref[1:2, :])


def _combine(slot_tiles, h1, w, mod, g23, sg, su, sd, ys, n_ctx):
    n, d = h1.shape
    t = ROUTE_TILE
    c = ys.shape[1]
    ff = sg.shape[1]
    ctx_tiles = n_ctx // t
    return pl.pallas_call(
        _combine_kernel,
        grid=(n // t,),
        in_specs=[
            pl.BlockSpec(memory_space=pl.ANY),
            pl.BlockSpec((t, d), lambda i: (i, 0)),
            pl.BlockSpec((TOP_K, t), lambda i: (0, i)),
            pl.BlockSpec((1, 6, d), lambda i: (jnp.where(i < ctx_tiles, 1, 0), 0, 0)),
            pl.BlockSpec((2, d), lambda i: (0, 0)),
            pl.BlockSpec((d, ff), lambda i: (0, 0)),
            pl.BlockSpec((d, ff), lambda i: (0, 0)),
            pl.BlockSpec((ff, d), lambda i: (0, 0)),
            pl.BlockSpec(memory_space=pl.ANY),
        ],
        out_specs=pl.BlockSpec((t, d), lambda i: (i, 0)),
        out_shape=jax.ShapeDtypeStruct((n, d), F32),
        scratch_shapes=[
            pltpu.SMEM((TOP_K * t,), I32),
            pltpu.VMEM((TOP_K, t, c), U32),
            pltpu.SemaphoreType.DMA,
            pltpu.SemaphoreType.DMA,
        ],
        compiler_params=_params("arbitrary"),
        name="combine",
    )(slot_tiles, h1, w, mod, g23, sg, su, sd, ys)


def _rope_tables(n_ctx, seq):
    nf = HEAD_DIM // 4
    inv = ROPE_THETA ** (-jnp.arange(nf, dtype=F32) / nf)
    pos = jnp.arange(seq, dtype=I32)
    row_ang = (pos // GRID_W).astype(F32)[:, None] * inv
    col_ang = (pos % GRID_W).astype(F32)[:, None] * inv
    cos = jnp.concatenate([jnp.cos(row_ang)] * 2 + [jnp.cos(col_ang)] * 2, axis=1)
    sin = jnp.concatenate([-jnp.sin(row_ang), jnp.sin(row_ang), -jnp.sin(col_ang), jnp.sin(col_ang)], axis=1)
    cos = jnp.concatenate([jnp.ones((n_ctx, HEAD_DIM), F32), cos], axis=0)
    sin = jnp.concatenate([jnp.zeros((n_ctx, HEAD_DIM), F32), sin], axis=0)
    return cos, sin


def _split_router(router):
    hi = router.astype(BF16)
    lo = (router - hi.astype(F32)).astype(BF16)
    return jnp.concatenate([hi, lo], axis=1)


def _moe(layer, h1, fp, logits, mod, g23, bias, wg, wu, wd, sg, su, sd, n_ctx):
    n = h1.shape[0]
    t = ROUTE_TILE
    tm = EXPERT_TILE
    top_e, w, rank, cnt = _route(logits, bias.reshape(N_EXPERTS, 1))
    counts = cnt[:, 0].astype(I32)
    padded = (counts + tm - 1) // tm * tm
    pad_end = jnp.cumsum(padded)
    pad_start = pad_end - padded
    onehot = top_e[:, :, None] == jnp.arange(N_EXPERTS, dtype=I32)
    slot = rank + jnp.sum(jnp.where(onehot, pad_start, 0), axis=-1)
    slot_tiles = slot.reshape(TOP_K, n // t, t).transpose(1, 0, 2).reshape(n // t, TOP_K * t)
    cap = n * TOP_K + N_EXPERTS * tm
    nb = cap // tm
    n_tiles = (pad_end[-1] // tm).astype(I32).reshape(1)
    tile_row = jnp.minimum(jnp.arange(nb, dtype=I32), n_tiles[0] - 1) * tm
    block_e = jnp.sum((pad_end[None, :] <= tile_row[:, None]).astype(I32), axis=1)
    block_e = jnp.minimum(block_e, N_EXPERTS - 1).astype(I32)
    meta = jnp.stack([pad_start, counts, padded]).astype(I32)
    xs = _dispatch(meta, n_tiles, slot_tiles, fp, cap)
    ys = _experts(layer, block_e, n_tiles, xs, wg, wu, wd)
    return _combine(slot_tiles, h1, w, mod, g23, sg, su, sd, ys, n_ctx)


def kernel(x, c, ctx, c_ctx, ada_w, ada_b, norm_g, ev_w_in, ev_w_out, ev_q_norm, ev_k_norm, ev_dw_w, ev_dw_b,
           ev_ln_g, ev_ln_b, od_w_in, od_w_out, od_lambda, od_subln_g, moe_router, moe_bias, moe_w_gate,
           moe_w_up, moe_w_down, sh_w_gate, sh_w_up, sh_w_down):
    b, seq, d = x.shape
    n_ctx = ctx.shape[1]
    assert b == 1 and n_ctx % ROW_TILE == 0 and seq % ROW_TILE == 0 and seq % GRID_W == 0
    depth = ada_w.shape[0]
    assert depth == 2
    cos, sin = _rope_tables(n_ctx, seq)
    cvec = jnp.stack([c[0], c_ctx], axis=1)
    mod_all = _ada(cvec, ada_w, ada_b).reshape(depth, 2, 6, d)
    h = jnp.concatenate([ctx[0], x[0]], axis=0)
    qscale = math.log2(math.e) * HEAD_DIM ** -0.5
    c_conv = ev_dw_w.shape[-1]
    wg, wu, wd = moe_w_gate, moe_w_up, moe_w_down
    sg, su, sd = sh_w_gate.astype(BF16), sh_w_up.astype(BF16), sh_w_down.astype(BF16)

    mod = mod_all[0]
    gains = norm_g[0]
    nq = A_HEADS * HEAD_DIM
    nkv = 2 * A_KV_HEADS * HEAD_DIM
    w_in = jnp.concatenate([ev_w_in[0][:, :nq], ev_w_in[0][:, nq + nkv:], ev_w_in[0][:, nq:nq + nkv]], axis=1)
    qkvu = _modmm(h, mod, gains[0:1], w_in.astype(BF16), n_ctx, col_tile=w_in.shape[1] // 2)
    u_col0 = nq
    k_col0 = nq + 2 * c_conv
    v_col0 = k_col0 + A_KV_HEADS * HEAD_DIM
    q, qmax = _prep(qkvu, 0, 0, A_HEADS, cos, sin, ev_q_norm[0].reshape(1, HEAD_DIM), True, qscale, False)
    kt, kmax = _prep(qkvu, 0, k_col0, A_KV_HEADS, cos, sin, ev_k_norm[0].reshape(1, HEAD_DIM), True, 1.0, True)
    o_attn = _gqa_attention(q, qmax, kt, kmax, qkvu, v_col0, n_ctx)
    o_conv = _conformer_conv(qkvu, u_col0, c_conv, ev_dw_w[0], ev_dw_b[0].reshape(1, -1),
                             ev_ln_g[0].reshape(1, -1), ev_ln_b[0].reshape(1, -1), n_ctx)
    h1, fp, logits = _post([o_attn, o_conv], ev_w_out[0].astype(BF16), h, 0, mod, gains[1:3],
                           _split_router(moe_router[0]), n_ctx)
    h = _moe(0, h1, fp, logits, mod, gains[2:4], moe_bias[0], wg, wu, wd, sg[0], su[0], sd[0], n_ctx)

    mod = mod_all[1]
    gains = norm_g[1]
    lam_init = 0.8 - 0.6 * math.exp(-0.3 * 1)
    lp = od_lambda[0].astype(F32)
    lam = (jnp.exp(jnp.sum(lp[0] * lp[1])) - jnp.exp(jnp.sum(lp[2] * lp[3])) + lam_init).reshape(1)
    qkv = _modmm(h, mod, gains[0:1], od_w_in[0].astype(BF16), n_ctx, col_tile=od_w_in.shape[2] // 4)
    n_qk = 2 * DIFF_HEADS
    ones = jnp.ones((1, HEAD_DIM), F32)
    q, qmax = _prep(qkv, n_ctx, 0, n_qk, cos, sin, ones, False, qscale, False)
    kt, kmax = _prep(qkv, 0, n_qk * HEAD_DIM, n_qk, cos, sin, ones, False, 1.0, True)
    o_attn = _diff_attention(lam, q, qmax, kt, kmax, qkv, 2 * n_qk * HEAD_DIM, od_subln_g[0].reshape(1, -1),
                             1.0 - lam_init, n_ctx)
    h1, fp, logits = _post([o_attn], od_w_out[0].astype(BF16), h, n_ctx, mod, gains[1:3],
                           _split_router(moe_router[1]), n_ctx)
    h = _moe(1, h1, fp, logits, mod, gains[2:4], moe_bias[1], wg, wu, wd, sg[1], su[1], sd[1], 0)
    return h[None]
```

```python
import functools
import math

import jax
import jax.numpy as jnp
from jax import lax
from jax.experimental import pallas as pl
from jax.experimental.pallas import tpu as pltpu

F32 = jnp.float32
BF16 = jnp.bfloat16
U32 = jnp.uint32
I32 = jnp.int32

GRID_W = 64
HEAD_DIM = 128
ROPE_THETA = 10000.0
EPS = 1e-6
A_HEADS = 8
A_KV_HEADS = 2
CONV_W = 31
DIFF_HEADS = 8
N_EXPERTS = 64
TOP_K = 8
N_GROUPS = 8
TOPK_GROUPS = 4
ROUTED_SCALE = 2.5

LANES = 128
SUBLANES = 8
VMEM_LIMIT_BYTES = 56 * 1024 * 1024

ROW_TILE = 256
KV_CHUNK = 256
LAT_SUB = 8
Q_TILE_GQA = 256
Q_TILE_DIFF = 1024
ROUTE_TILE = 256
EXPERT_TILE = 256
CONV_HALO = 16
NEG_BIG = -1e30
FAST_BOUND_MAX = 60.0


def _params(*sem):
    return pltpu.CompilerParams(dimension_semantics=sem, vmem_limit_bytes=VMEM_LIMIT_BYTES)


def _sigmoid(x):
    return 1.0 / (1.0 + jnp.exp(-x))


def _rms(x, g):
    return x * lax.rsqrt(jnp.mean(x * x, axis=-1, keepdims=True) + EPS) * g


def _modulate(x, g, shift, scale):
    return _rms(x, g) * (1.0 + scale) + shift


def _pack_halves(x):
    c = x.shape[1] // 2
    lo = lax.bitcast_convert_type(x[:, :c].astype(BF16).astype(F32), U32)
    hi = lax.bitcast_convert_type(x[:, c:].astype(BF16).astype(F32), U32)
    return (hi & jnp.uint32(0xFFFF0000)) | (lo >> 16)


def _unpack_halves(w):
    lo = lax.bitcast_convert_type(w << 16, F32)
    hi = lax.bitcast_convert_type(w & jnp.uint32(0xFFFF0000), F32)
    return lo, hi


ADA_COLS = 512
ADA_KCHUNK = 64


def _ada_kernel(cv_ref, w_ref, b_ref, o_ref, s_ref):
    cv = cv_ref[...]
    s_ref[...] = cv * _sigmoid(cv)
    d = cv.shape[0]
    tn = o_ref.shape[-1]

    def body(k, acc):
        a0, a1 = acc
        r0 = pl.multiple_of(k * ADA_KCHUNK, ADA_KCHUNK)
        w = w_ref[0, pl.ds(r0, ADA_KCHUNK), :]
        s = s_ref[pl.ds(r0, ADA_KCHUNK), :]
        p0 = (w * s[:, 0:1]).reshape(ADA_KCHUNK // SUBLANES, SUBLANES, tn)
        p1 = (w * s[:, 1:2]).reshape(ADA_KCHUNK // SUBLANES, SUBLANES, tn)
        return a0 + jnp.sum(p0, axis=0), a1 + jnp.sum(p1, axis=0)

    z = jnp.zeros((SUBLANES, tn), F32)
    a0, a1 = lax.fori_loop(0, d // ADA_KCHUNK, body, (z, z))
    b = b_ref[0]
    o_ref[0, 0:1, :] = jnp.sum(a0, axis=0, keepdims=True) + b
    o_ref[0, 1:2, :] = jnp.sum(a1, axis=0, keepdims=True) + b


def _ada(cvec, ada_w, ada_b):
    depth, d, n = ada_w.shape
    return pl.pallas_call(
        _ada_kernel,
        grid=(depth, n // ADA_COLS),
        in_specs=[
            pl.BlockSpec((d, 2), lambda l, j: (0, 0)),
            pl.BlockSpec((1, d, ADA_COLS), lambda l, j: (l, 0, j)),
            pl.BlockSpec((1, 1, ADA_COLS), lambda l, j: (l, 0, j)),
        ],
        out_specs=pl.BlockSpec((1, 2, ADA_COLS), lambda l, j: (l, 0, j)),
        out_shape=jax.ShapeDtypeStruct((depth, 2, n), F32),
        scratch_shapes=[pltpu.VMEM((d, 2), F32)],
        compiler_params=_params("arbitrary", "arbitrary"),
        name="ada",
    )(cvec, ada_w, ada_b.reshape(depth, 1, n))


def _modmm_kernel(h_ref, mod_ref, g_ref, w_ref, o_ref):
    a = _modulate(h_ref[...], g_ref[...], mod_ref[0, 0:1, :], mod_ref[0, 1:2, :])
    o_ref[...] = jnp.dot(a.astype(BF16), w_ref[...], preferred_element_type=F32).astype(o_ref.dtype)


def _modmm(h, mod, g, w, n_ctx, col_tile):
    r, d = h.shape
    n = w.shape[1]
    ctx_tiles = n_ctx // ROW_TILE
    return pl.pallas_call(
        _modmm_kernel,
        grid=(n // col_tile, r // ROW_TILE),
        in_specs=[
            pl.BlockSpec((ROW_TILE, d), lambda j, i: (i, 0)),
            pl.BlockSpec((1, 6, d), lambda j, i: (jnp.where(i < ctx_tiles, 1, 0), 0, 0)),
            pl.BlockSpec((1, d), lambda j, i: (0, 0)),
            pl.BlockSpec((d, col_tile), lambda j, i: (0, j)),
        ],
        out_specs=pl.BlockSpec((ROW_TILE, col_tile), lambda j, i: (i, j)),
        out_shape=jax.ShapeDtypeStruct((r, n), BF16),
        compiler_params=_params("arbitrary", "arbitrary"),
        name="modmm",
    )(h, mod, g, w)


def _rope(x, c, s):
    lane = lax.broadcasted_iota(I32, x.shape, 1)
    first = (lane % 64) < 32
    partner = jnp.where(first, pltpu.roll(x, 96, axis=1), pltpu.roll(x, 32, axis=1))
    return x * c + partner * s


def _prep_kernel(x_ref, c_ref, s_ref, g_ref, o_ref, st_ref, *, n_heads, norm, scale, transpose):
    c = c_ref[...]
    s = s_ref[...]
    stats = []
    for h in range(n_heads):
        x = x_ref[:, h * HEAD_DIM:(h + 1) * HEAD_DIM].astype(F32)
        if norm:
            x = _rms(x, g_ref[...])
        y = (_rope(x, c, s) * scale).astype(BF16).astype(F32)
        sq = jnp.max(jnp.sum(y * y, axis=-1, keepdims=True), axis=0, keepdims=True)
        stats.append(jnp.broadcast_to(sq, (1, LANES)))
        if transpose:
            o_ref[h, 0] = y.T.astype(o_ref.dtype)
        else:
            o_ref[:, h * HEAD_DIM:(h + 1) * HEAD_DIM] = y.astype(o_ref.dtype)
    st_ref[0] = jnp.concatenate(stats, axis=0)


def _prep(x, row0, col0, n_heads, cos, sin, g, norm, scale, transpose):
    r = x.shape[0] - row0
    width = n_heads * HEAD_DIM
    cb = col0 // width
    tr = KV_CHUNK
    nt = r // tr
    rb = row0 // tr
    if transpose:
        o_spec = pl.BlockSpec((n_heads, 1, HEAD_DIM, tr), lambda i: (0, i, 0, 0))
        o_shape = jax.ShapeDtypeStruct((n_heads, nt, HEAD_DIM, tr), BF16)
    else:
        o_spec = pl.BlockSpec((tr, width), lambda i: (i, 0))
        o_shape = jax.ShapeDtypeStruct((r, width), BF16)
    out, stats = pl.pallas_call(
        functools.partial(_prep_kernel, n_heads=n_heads, norm=norm, scale=scale, transpose=transpose),
        grid=(nt,),
        in_specs=[
            pl.BlockSpec((tr, width), lambda i: (i + rb, cb)),
            pl.BlockSpec((tr, HEAD_DIM), lambda i: (i + rb, 0)),
            pl.BlockSpec((tr, HEAD_DIM), lambda i: (i + rb, 0)),
            pl.BlockSpec((1, HEAD_DIM), lambda i: (0, 0)),
        ],
        out_specs=[o_spec, pl.BlockSpec((1, n_heads, LANES), lambda i: (i, 0, 0))],
        out_shape=[o_shape, jax.ShapeDtypeStruct((nt, n_heads, LANES), F32)],
        compiler_params=_params("arbitrary"),
        name="prep_k" if transpose else "prep_q",
    )(x, cos, sin, g)
    return out, jnp.sqrt(jnp.max(stats[:, :, 0], axis=0))


def _lane_tile(x, n):
    return x if n == 1 else jnp.concatenate([x] * n, axis=1)


def _lane_fold(p):
    out = p[:, 0:LANES]
    for j in range(1, p.shape[1] // LANES):
        out = out + p[:, j * LANES:(j + 1) * LANES]
    return out


def _softmax_step(q, kt, v, m_ref, l_ref, acc_ref, online):
    s = jnp.dot(q, kt, preferred_element_type=F32)
    tiles = s.shape[1] // LANES
    if online:
        m_prev = m_ref[...]
        m_new = jnp.maximum(m_prev, jnp.max(s, axis=-1, keepdims=True))
        alpha = jnp.exp2(m_prev - m_new)
        p = jnp.exp2(s - _lane_tile(m_new, tiles))
        l_ref[...] = alpha * l_ref[...] + _lane_fold(p)
        acc_ref[...] = (_lane_tile(alpha, acc_ref.shape[1] // LANES) * acc_ref[...]
                        + jnp.dot(p.astype(BF16), v, preferred_element_type=F32))
        m_ref[...] = m_new
    else:
        p = jnp.exp2(s - _lane_tile(m_ref[...], tiles))
        l_ref[...] += _lane_fold(p)
        acc_ref[...] += jnp.dot(p.astype(BF16), v, preferred_element_type=F32)


def _scan_keys(kt_ref, n_k, v_ref, n_ctx, n_lat, see_latent, step):
    for cc in range(n_ctx // KV_CHUNK):
        step([kt_ref[j, cc] for j in range(n_k)], v_ref[cc * KV_CHUNK:(cc + 1) * KV_CHUNK, :])
    ctx_chunks = n_ctx // KV_CHUNK

    def latent():
        def body(c, carry):
            base = ctx_chunks + c * LAT_SUB
            kts = [jnp.concatenate([kt_ref[j, base + u] for u in range(LAT_SUB)], axis=1) for j in range(n_k)]
            r0 = pl.multiple_of(n_ctx + c * (LAT_SUB * KV_CHUNK), KV_CHUNK)
            step(kts, v_ref[pl.ds(r0, LAT_SUB * KV_CHUNK), :])
            return carry

        lax.fori_loop(0, n_lat, body, 0)

    if see_latent is None:
        latent()
    else:
        pl.when(see_latent)(latent)


def _row_bound(q, kmax):
    qf = q.astype(F32)
    return jnp.sqrt(jnp.sum(qf * qf, axis=-1, keepdims=True)) * kmax


def _gqa_kernel(fast_ref, kmax_ref, q_ref, kt_ref, v_ref, o_ref, qs_ref, m_ref, l_ref, acc_ref,
                *, ctx_tiles, n_ctx, n_lat):
    g = pl.program_id(0)
    tq = q_ref.shape[0]
    group = q_ref.shape[1] // HEAD_DIM
    for u in range(group):
        qs_ref[u * tq:(u + 1) * tq, :] = q_ref[:, u * HEAD_DIM:(u + 1) * HEAD_DIM]
    l_ref[...] = jnp.zeros(l_ref.shape, F32)
    acc_ref[...] = jnp.zeros(acc_ref.shape, F32)
    see_latent = pl.program_id(1) >= ctx_tiles
    fast = fast_ref[g] == 1

    def scan(online):
        def step(kts, v):
            _softmax_step(qs_ref[...], kts[0], v, m_ref, l_ref, acc_ref, online)

        _scan_keys(kt_ref, 1, v_ref, n_ctx, n_lat, see_latent, step)

    @pl.when(fast)
    def _():
        m_ref[...] = jnp.broadcast_to(_row_bound(qs_ref[...], kmax_ref[g]), m_ref.shape)
        scan(False)

    @pl.when(jnp.logical_not(fast))
    def _():
        m_ref[...] = jnp.full(m_ref.shape, NEG_BIG, F32)
        scan(True)

    o = acc_ref[...] / jnp.sum(l_ref[...], axis=-1, keepdims=True)
    for u in range(group):
        o_ref[:, u * HEAD_DIM:(u + 1) * HEAD_DIM] = o[u * tq:(u + 1) * tq, :].astype(o_ref.dtype)


def _kv_chunks(r, n_ctx):
    n_lat_chunks = (r - n_ctx) // KV_CHUNK
    assert n_ctx % KV_CHUNK == 0 and n_lat_chunks % LAT_SUB == 0
    return n_lat_chunks // LAT_SUB


def _gqa_attention(q, qmax, kt, kmax, qkvu, v_col0, n_ctx):
    r = q.shape[0]
    group = A_HEADS // A_KV_HEADS
    tq = Q_TILE_GQA
    nc = r // KV_CHUNK
    vb = v_col0 // HEAD_DIM
    fast = (jnp.max(qmax.reshape(A_KV_HEADS, group), axis=1) * kmax <= FAST_BOUND_MAX).astype(I32)
    kern = functools.partial(_gqa_kernel, ctx_tiles=n_ctx // tq, n_ctx=n_ctx, n_lat=_kv_chunks(r, n_ctx))
    smem = pl.BlockSpec(memory_space=pltpu.SMEM)
    return pl.pallas_call(
        kern,
        grid=(A_KV_HEADS, r // tq),
        in_specs=[
            smem, smem,
            pl.BlockSpec((tq, group * HEAD_DIM), lambda g, i: (i, g)),
            pl.BlockSpec((1, nc, HEAD_DIM, KV_CHUNK), lambda g, i: (g, 0, 0, 0)),
            pl.BlockSpec((r, HEAD_DIM), lambda g, i: (0, vb + g)),
        ],
        out_specs=pl.BlockSpec((tq, group * HEAD_DIM), lambda g, i: (i, g)),
        out_shape=jax.ShapeDtypeStruct((r, A_HEADS * HEAD_DIM), BF16),
        scratch_shapes=[
            pltpu.VMEM((group * tq, HEAD_DIM), BF16),
            pltpu.VMEM((group * tq, LANES), F32),
            pltpu.VMEM((group * tq, LANES), F32),
            pltpu.VMEM((group * tq, HEAD_DIM), F32),
        ],
        compiler_params=_params("arbitrary", "arbitrary"),
        name="gqa_attn",
    )(fast, kmax, q, kt, qkvu)


def _diff_kernel(lam_ref, fast_ref, kmax_ref, q_ref, kt_ref, v_ref, g_ref, o_ref, m0, l0, a0, m1, l1, a1,
                 *, out_scale, n_ctx, n_lat):
    h = pl.program_id(0)
    comps = ((m0, l0, a0), (m1, l1, a1))
    for _, l_ref, a_ref in comps:
        l_ref[...] = jnp.zeros(l_ref.shape, F32)
        a_ref[...] = jnp.zeros(a_ref.shape, F32)
    fast = fast_ref[h] == 1

    def q_comp(c):
        return q_ref[:, c * HEAD_DIM:(c + 1) * HEAD_DIM]

    def scan(online):
        def step(kts, v):
            for c, (m_ref, l_ref, a_ref) in enumerate(comps):
                _softmax_step(q_comp(c), kts[c], v, m_ref, l_ref, a_ref, online)

        _scan_keys(kt_ref, 2, v_ref, n_ctx, n_lat, None, step)

    @pl.when(fast)
    def _():
        for c, (m_ref, _, _) in enumerate(comps):
            m_ref[...] = jnp.broadcast_to(_row_bound(q_comp(c), kmax_ref[2 * h + c]), m_ref.shape)
        scan(False)

    @pl.when(jnp.logical_not(fast))
    def _():
        for m_ref, _, _ in comps:
            m_ref[...] = jnp.full(m_ref.shape, NEG_BIG, F32)
        scan(True)

    o = (a0[...] / jnp.sum(l0[...], axis=-1, keepdims=True)
         - lam_ref[0] * (a1[...] / jnp.sum(l1[...], axis=-1, keepdims=True)))
    o_ref[...] = (_rms(o, g_ref[...]) * out_scale).astype(o_ref.dtype)


def _diff_attention(lam, q, qmax, kt, kmax, qkv, v_col0, subln_g, out_scale, n_ctx):
    r = qkv.shape[0]
    seq = q.shape[0]
    tq = math.gcd(Q_TILE_DIFF, seq)
    nc = r // KV_CHUNK
    dv = 2 * HEAD_DIM
    vb = v_col0 // dv
    fast = (jnp.max((qmax * kmax).reshape(DIFF_HEADS, 2), axis=1) <= FAST_BOUND_MAX).astype(I32)
    kern = functools.partial(_diff_kernel, out_scale=out_scale, n_ctx=n_ctx, n_lat=_kv_chunks(r, n_ctx))
    stat = pltpu.VMEM((tq, LANES), F32)
    acc = pltpu.VMEM((tq, dv), F32)
    smem = pl.BlockSpec(memory_space=pltpu.SMEM)
    return pl.pallas_call(
        kern,
        grid=(DIFF_HEADS, seq // tq),
        in_specs=[
            smem, smem, smem,
            pl.BlockSpec((tq, dv), lambda h, i: (i, h)),
            pl.BlockSpec((2, nc, HEAD_DIM, KV_CHUNK), lambda h, i: (h, 0, 0, 0), pipeline_mode=pl.Buffered(1)),
            pl.BlockSpec((r, dv), lambda h, i: (0, vb + h), pipeline_mode=pl.Buffered(1)),
            pl.BlockSpec((1, dv), lambda h, i: (0, 0)),
        ],
        out_specs=pl.BlockSpec((tq, dv), lambda h, i: (i, h)),
        out_shape=jax.ShapeDtypeStruct((seq, DIFF_HEADS * dv), BF16),
        scratch_shapes=[stat, stat, acc, stat, stat, acc],
        compiler_params=_params("arbitrary", "arbitrary"),
        name="diff_attn",
    )(lam, fast, kmax, q, kt, qkv, subln_g)


def _conv_kernel(ap_ref, gp_ref, ac_ref, gc_ref, an_ref, gn_ref, w_ref, b_ref, lg_ref, lb_ref, o_ref, h_ref, sh_ref,
                 *, seq_starts, seq_ends):
    i = pl.program_id(0)
    tr = ac_ref.shape[0]
    is_start = functools.reduce(jnp.logical_or, [i == s for s in seq_starts])
    is_end = functools.reduce(jnp.logical_or, [i == e for e in seq_ends])

    def glu(a_ref, g_ref):
        return a_ref[...].astype(F32) * _sigmoid(g_ref[...].astype(F32))

    h_ref[0:CONV_HALO, :] = jnp.where(is_start, 0.0, glu(ap_ref, gp_ref))
    h_ref[CONV_HALO:CONV_HALO + tr, :] = glu(ac_ref, gc_ref)
    h_ref[CONV_HALO + tr:, :] = jnp.where(is_end, 0.0, glu(an_ref, gn_ref))
    rows = sh_ref.shape[1]
    for b in range(1, SUBLANES):
        sh_ref[b - 1] = h_ref[b:b + rows, :]
    acc = jnp.zeros((tr, ac_ref.shape[1]), F32)
    base = CONV_HALO - CONV_W // 2
    for j in range(CONV_W):
        a, b = divmod(base + j, SUBLANES)
        r0 = a * SUBLANES
        src = h_ref[r0:r0 + tr, :] if b == 0 else sh_ref[b - 1, r0:r0 + tr, :]
        acc = acc + w_ref[j:j + 1, :] * src
    y = acc + b_ref[...]
    mu = jnp.mean(y, axis=-1, keepdims=True)
    var = jnp.mean(jnp.square(y - mu), axis=-1, keepdims=True)
    z = (y - mu) * lax.rsqrt(var + EPS) * lg_ref[...] + lb_ref[...]
    o_ref[...] = (z * _sigmoid(z)).astype(o_ref.dtype)


CONV_TILE = 128


def _conformer_conv(qkvu, u_col0, c_conv, dw_w, dw_b, ln_g, ln_b, n_ctx):
    r = qkvu.shape[0]
    tr = CONV_TILE
    nt = r // tr
    hb = tr // CONV_HALO
    ab = u_col0 // c_conv
    gb = ab + 1
    last_halo = r // CONV_HALO - 1
    seq_starts = (0, n_ctx // tr)
    seq_ends = (n_ctx // tr - 1, nt - 1)

    def prev(col):
        return pl.BlockSpec((CONV_HALO, c_conv), lambda i: (jnp.maximum(i * hb - 1, 0), col))

    def cur(col):
        return pl.BlockSpec((tr, c_conv), lambda i: (i, col))

    def nxt(col):
        return pl.BlockSpec((CONV_HALO, c_conv), lambda i: (jnp.minimum((i + 1) * hb, last_halo), col))

    vec = pl.BlockSpec((1, c_conv), lambda i: (0, 0))
    return pl.pallas_call(
        functools.partial(_conv_kernel, seq_starts=seq_starts, seq_ends=seq_ends),
        grid=(nt,),
        in_specs=[prev(ab), prev(gb), cur(ab), cur(gb), nxt(ab), nxt(gb),
                  pl.BlockSpec((CONV_W, c_conv), lambda i: (0, 0)), vec, vec, vec],
        out_specs=pl.BlockSpec((tr, c_conv), lambda i: (i, 0)),
        out_shape=jax.ShapeDtypeStruct((r, c_conv), BF16),
        scratch_shapes=[pltpu.VMEM((tr + 2 * CONV_HALO, c_conv), F32),
                        pltpu.VMEM((SUBLANES - 1, tr + 2 * CONV_HALO - SUBLANES, c_conv), F32)],
        compiler_params=_params("arbitrary"),
        name="conformer_conv",
    )(qkvu, qkvu, qkvu, qkvu, qkvu, qkvu, dw_w, dw_b, ln_g, ln_b)


def _post_kernel(*refs, n_in):
    o_refs = refs[:n_in]
    w_ref, h_ref, mod_ref, g_ref, rr_ref, h1_ref, fp_ref, lg_ref = refs[n_in:]
    k0 = 0
    out = None
    for o_in in o_refs:
        kw = o_in.shape[1]
        part = jnp.dot(o_in[...], w_ref[k0:k0 + kw, :], preferred_element_type=F32)
        out = part if out is None else out + part
        k0 += kw
    h1 = h_ref[...] + mod_ref[0, 2:3, :] * _rms(out, g_ref[0:1, :])
    h1_ref[...] = h1
    f = _modulate(h1, g_ref[1:2, :], mod_ref[0, 3:4, :], mod_ref[0, 4:5, :])
    fp_ref[...] = _pack_halves(f)
    f_hi = f.astype(BF16)
    f_lo = (f - f_hi.astype(F32)).astype(BF16)
    l2 = (jnp.dot(f_hi, rr_ref[...], preferred_element_type=F32)
          + jnp.dot(f_lo, rr_ref[...], preferred_element_type=F32))
    lg_ref[...] = l2 + pltpu.roll(l2, N_EXPERTS, axis=1)


def _post(o_parts, w_out, h, row_off, mod, g12, rr, n_ctx):
    n = o_parts[0].shape[0]
    d = h.shape[1]
    tm = ROW_TILE
    off = row_off // tm
    ctx_tiles = (n_ctx - row_off) // tm
    in_specs = [pl.BlockSpec((tm, o.shape[1]), lambda i: (i, 0)) for o in o_parts]
    in_specs += [
        pl.BlockSpec((d, d), lambda i: (0, 0)),
        pl.BlockSpec((tm, d), lambda i: (i + off, 0)),
        pl.BlockSpec((1, 6, d), lambda i: (jnp.where(i < ctx_tiles, 1, 0), 0, 0)),
        pl.BlockSpec((2, d), lambda i: (0, 0)),
        pl.BlockSpec((d, 2 * N_EXPERTS), lambda i: (0, 0)),
    ]
    return pl.pallas_call(
        functools.partial(_post_kernel, n_in=len(o_parts)),
        grid=(n // tm,),
        in_specs=in_specs,
        out_specs=[
            pl.BlockSpec((tm, d), lambda i: (i, 0)),
            pl.BlockSpec((tm, d // 2), lambda i: (i, 0)),
            pl.BlockSpec((tm, 2 * N_EXPERTS), lambda i: (i, 0)),
        ],
        out_shape=[
            jax.ShapeDtypeStruct((n, d), F32),
            jax.ShapeDtypeStruct((n, d // 2), U32),
            jax.ShapeDtypeStruct((n, 2 * N_EXPERTS), F32),
        ],
        compiler_params=_params("arbitrary"),
        name="post_mixer",
    )(*o_parts, w_out, h, mod, g12, rr)


def _route_kernel(lg_ref, bias_ref, e_ref, w_ref, rank_ref, cnt_ref):
    t = lg_ref.shape[0]
    gsz = N_EXPERTS // N_GROUPS

    @pl.when(pl.program_id(0) == 0)
    def _():
        cnt_ref[...] = jnp.zeros(cnt_ref.shape, F32)

    logits = lg_ref[...].T[0:N_EXPERTS, :]
    scores = _sigmoid(logits)
    choice = scores + bias_ref[...]
    sub = lax.broadcasted_iota(I32, (gsz, t), 0).astype(F32)
    grp_rows = []
    for g in range(N_GROUPS):
        cg = choice[g * gsz:(g + 1) * gsz, :]
        m1 = jnp.max(cg, axis=0, keepdims=True)
        i1 = jnp.min(jnp.where(cg == m1, sub, float(gsz)), axis=0, keepdims=True)
        m2 = jnp.max(jnp.where(sub == i1, -jnp.inf, cg), axis=0, keepdims=True)
        grp_rows.append(m1 + m2)
    grp = jnp.concatenate(grp_rows, axis=0)
    gidx = lax.broadcasted_iota(I32, (N_GROUPS, t), 0).astype(F32)
    gsel = jnp.zeros((N_GROUPS, t), F32)
    for _ in range(TOPK_GROUPS):
        gm = jnp.max(grp, axis=0, keepdims=True)
        gi = jnp.min(jnp.where(grp == gm, gidx, float(N_GROUPS)), axis=0, keepdims=True)
        hit = gidx == gi
        gsel = jnp.where(hit, 1.0, gsel)
        grp = jnp.where(hit, -jnp.inf, grp)
    eidx = lax.broadcasted_iota(I32, (N_EXPERTS, t), 0).astype(F32)
    mc_rows = []
    for g in range(N_GROUPS):
        mc_rows.append(jnp.where(gsel[g:g + 1, :] > 0.5, choice[g * gsz:(g + 1) * gsz, :], -jnp.inf))
    mc = jnp.concatenate(mc_rows, axis=0)
    sel_all = jnp.zeros((N_EXPERTS, t), F32)
    e_rows, w_rows = [], []
    for _ in range(TOP_K):
        mx = jnp.max(mc, axis=0, keepdims=True)
        ei = jnp.min(jnp.where(mc == mx, eidx, float(N_EXPERTS)), axis=0, keepdims=True)
        hit = eidx == ei
        e_rows.append(ei)
        w_rows.append(jnp.sum(jnp.where(hit, scores, 0.0), axis=0, keepdims=True))
        sel_all = jnp.where(hit, 1.0, sel_all)
        mc = jnp.where(hit, -jnp.inf, mc)
    e_all = jnp.concatenate(e_rows, axis=0)
    w_all = jnp.concatenate(w_rows, axis=0)
    w_ref[...] = w_all / jnp.sum(w_all, axis=0, keepdims=True) * ROUTED_SCALE
    e_ref[...] = e_all.astype(I32)
    src = lax.broadcasted_iota(I32, (t, t), 0)
    dst = lax.broadcasted_iota(I32, (t, t), 1)
    upper = jnp.where(src < dst, 1.0, 0.0).astype(BF16)
    pos = jnp.dot(sel_all.astype(BF16), upper, preferred_element_type=F32) + cnt_ref[:, 0:1]
    r_rows = [jnp.sum(jnp.where(eidx == e_rows[k], pos, 0.0), axis=0, keepdims=True) for k in range(TOP_K)]
    rank_ref[...] = jnp.concatenate(r_rows, axis=0).astype(I32)
    cnt_ref[...] = cnt_ref[...] + jnp.sum(sel_all, axis=1, keepdims=True)


def _route(logits, bias):
    n = logits.shape[0]
    t = ROUTE_TILE
    row = pl.BlockSpec((TOP_K, t), lambda i: (0, i))
    return pl.pallas_call(
        _route_kernel,
        grid=(n // t,),
        in_specs=[
            pl.BlockSpec((t, 2 * N_EXPERTS), lambda i: (i, 0)),
            pl.BlockSpec((N_EXPERTS, 1), lambda i: (0, 0)),
        ],
        out_specs=[row, row, row, pl.BlockSpec((N_EXPERTS, LANES), lambda i: (0, 0))],
        out_shape=[
            jax.ShapeDtypeStruct((TOP_K, n), I32),
            jax.ShapeDtypeStruct((TOP_K, n), F32),
            jax.ShapeDtypeStruct((TOP_K, n), I32),
            jax.ShapeDtypeStruct((N_EXPERTS, LANES), F32),
        ],
        compiler_params=_params("arbitrary"),
        name="route",
    )(logits, bias)


def _dispatch_kernel(meta_ref, nt_ref, slot_hbm, fp_ref, xs_hbm, slot_smem, zero_ref, sem_idx, sem_rows, sem_zero):
    i = pl.program_id(0)
    t = fp_ref.shape[0]
    idx_copy = pltpu.make_async_copy(slot_hbm.at[i], slot_smem, sem_idx)
    idx_copy.start()
    idx_copy.wait()

    def issue(tok, carry):
        for k in range(TOP_K):
            s = slot_smem[k * t + tok]
            pltpu.make_async_copy(fp_ref.at[pl.ds(tok, 1), :], xs_hbm.at[pl.ds(s, 1), :],
                                  sem_rows).start(priority=k % 2)
        return carry

    lax.fori_loop(0, t, issue, 0)
    for _ in range(TOP_K):
        pltpu.make_async_copy(fp_ref, xs_hbm.at[pl.ds(0, t), :], sem_rows).wait()

    @pl.when(i == pl.num_programs(0) - 1)
    def _():
        tm = zero_ref.shape[0]
        n_all = xs_hbm.shape[0] // tm
        zero_ref[...] = jnp.zeros(zero_ref.shape, zero_ref.dtype)

        def pad_row(row):
            return pltpu.make_async_copy(zero_ref.at[pl.ds(0, 1), :], xs_hbm.at[pl.ds(row, 1), :], sem_zero)

        def tail_tile(b):
            return pltpu.make_async_copy(zero_ref, xs_hbm.at[pl.ds(pl.multiple_of(b * tm, tm), tm), :], sem_zero)

        def per_expert(e, total):
            first = meta_ref[0, e] + meta_ref[1, e]
            n_pad = meta_ref[2, e] - meta_ref[1, e]

            def one(r, carry):
                pad_row(first + r).start()
                return carry

            lax.fori_loop(0, n_pad, one, 0)
            return total + n_pad

        total = lax.fori_loop(0, N_EXPERTS, per_expert, 0)

        def start_tail(b, carry):
            tail_tile(b).start()
            return carry

        lax.fori_loop(nt_ref[0], n_all, start_tail, 0)

        def wait_row(r, carry):
            pad_row(0).wait()
            return carry

        lax.fori_loop(0, total, wait_row, 0)

        def wait_tail(b, carry):
            tail_tile(b).wait()
            return carry

        lax.fori_loop(nt_ref[0], n_all, wait_tail, 0)


def _dispatch(meta, n_tiles, slot_tiles, fp, cap):
    n, c = fp.shape
    t = ROUTE_TILE
    smem = pl.BlockSpec(memory_space=pltpu.SMEM)
    return pl.pallas_call(
        _dispatch_kernel,
        grid=(n // t,),
        in_specs=[
            smem, smem,
            pl.BlockSpec(memory_space=pl.ANY),
            pl.BlockSpec((t, c), lambda i: (i, 0)),
        ],
        out_specs=pl.BlockSpec(memory_space=pl.ANY),
        out_shape=jax.ShapeDtypeStruct((cap, c), U32),
        scratch_shapes=[
            pltpu.SMEM((TOP_K * t,), I32),
            pltpu.VMEM((EXPERT_TILE, c), U32),
            pltpu.SemaphoreType.DMA, pltpu.SemaphoreType.DMA, pltpu.SemaphoreType.DMA,
        ],
        compiler_params=_params("arbitrary"),
        name="dispatch",
    )(meta, n_tiles, slot_tiles, fp)


def _experts_kernel(be_ref, nt_ref, xs_ref, wg_ref, wu_ref, wd_ref, ys_ref):
    del be_ref
    used = pl.program_id(0) < nt_ref[0]

    @pl.when(jnp.logical_not(used))
    def _():
        ys_ref[...] = jnp.zeros(ys_ref.shape, ys_ref.dtype)

    @pl.when(used)
    def _():
        lo, hi = _unpack_halves(xs_ref[...])
        lo = lo.astype(BF16)
        hi = hi.astype(BF16)
        c = lo.shape[1]
        def proj(w_ref):
            return (jnp.dot(lo, w_ref[0, 0, 0:c, :].astype(BF16), preferred_element_type=F32)
                    + jnp.dot(hi, w_ref[0, 0, c:, :].astype(BF16), preferred_element_type=F32))

        g = proj(wg_ref)
        u = proj(wu_ref)
        a = (g * _sigmoid(g) * u).astype(BF16)
        ys_ref[...] = _pack_halves(jnp.dot(a, wd_ref[0, 0].astype(BF16), preferred_element_type=F32))


def _experts(layer, block_e, n_tiles, xs, wg, wu, wd):
    cap, c = xs.shape
    _, e, d, ff = wg.shape
    tm = EXPERT_TILE

    def row_map(b, be, nt):
        return (jnp.minimum(b, nt[0] - 1), 0)

    grid_spec = pltpu.PrefetchScalarGridSpec(
        num_scalar_prefetch=2,
        grid=(cap // tm,),
        in_specs=[
            pl.BlockSpec((tm, c), row_map),
            pl.BlockSpec((1, 1, d, ff), lambda b, be, nt: (layer, be[b], 0, 0)),
            pl.BlockSpec((1, 1, d, ff), lambda b, be, nt: (layer, be[b], 0, 0)),
            pl.BlockSpec((1, 1, ff, d), lambda b, be, nt: (layer, be[b], 0, 0)),
        ],
        out_specs=pl.BlockSpec((tm, c), lambda b, be, nt: (b, 0)),
    )
    return pl.pallas_call(
        _experts_kernel,
        grid_spec=grid_spec,
        out_shape=jax.ShapeDtypeStruct((cap, c), U32),
        compiler_params=_params("arbitrary"),
        name="experts",
    )(block_e, n_tiles, xs, wg, wu, wd)


def _combine_kernel(slot_hbm, h_ref, w_ref, mod_ref, g_ref, sg_ref, su_ref, sd_ref, ys_hbm, o_ref,
                    slot_smem, ybuf, sem_idx, sem_rows):
    i = pl.program_id(0)
    t = h_ref.shape[0]

    def start_gather(tile, buf):
        base = buf * (TOP_K * t)
        idx_copy = pltpu.make_async_copy(slot_hbm.at[tile], slot_smem.at[pl.ds(base, TOP_K * t)], sem_idx)
        idx_copy.start()
        idx_copy.wait()

        def issue(tok, carry):
            for k in range(TOP_K):
                s = slot_smem[base + k * t + tok]
                pltpu.make_async_copy(ys_hbm.at[pl.ds(s, 1), :], ybuf.at[buf, k, pl.ds(tok, 1), :],
                                      sem_rows.at[buf]).start(priority=k % 2)
            return carry

        lax.fori_loop(0, t, issue, 0)

    @pl.when(i == 0)
    def _():
        start_gather(0, 0)

    cur = i % 2
    for buf in (0, 1):
        @pl.when(jnp.logical_and(i + 1 < pl.num_programs(0), cur != buf))
        def _():
            start_gather(i + 1, buf)

    h1 = h_ref[...]
    f = _modulate(h1, g_ref[0:1, :], mod_ref[0, 3:4, :], mod_ref[0, 4:5, :]).astype(BF16)
    sg = jnp.dot(f, sg_ref[...], preferred_element_type=F32)
    su = jnp.dot(f, su_ref[...], preferred_element_type=F32)
    y = jnp.dot((sg * _sigmoid(sg) * su).astype(BF16), sd_ref[...], preferred_element_type=F32)
    for k in range(TOP_K):
        pltpu.make_async_copy(ys_hbm.at[pl.ds(0, t), :], ybuf.at[cur, k], sem_rows.at[cur]).wait()
    c = ybuf.shape[3]
    wt = w_ref[...].T
    y_lo = y[:, 0:c]
    y_hi = y[:, c:]
    for k in range(TOP_K):
        lo, hi = _unpack_halves(ybuf[cur, k])
        wk = wt[:, k:k + 1]
        y_lo = y_lo + wk * lo
        y_hi = y_hi + wk * hi
    y = jnp.concatenate([y_lo, y_hi], axis=1)
    o_ref[...] = h1 + mod_ref[0, 5:6, :] * _rms(y, g_ref[1:2, :])


def _combine(slot_tiles, h1, w, mod, g23, sg, su, sd, ys, n_ctx):
    n, d = h1.shape
    t = ROUTE_TILE
    c = ys.shape[1]
    ff = sg.shape[1]
    ctx_tiles = n_ctx // t
    return pl.pallas_call(
        _combine_kernel,
        grid=(n // t,),
        in_specs=[
            pl.BlockSpec(memory_space=pl.ANY),
            pl.BlockSpec((t, d), lambda i: (i, 0)),
            pl.BlockSpec((TOP_K, t), lambda i: (0, i)),
            pl.BlockSpec((1, 6, d), lambda i: (jnp.where(i < ctx_tiles, 1, 0), 0, 0)),
            pl.BlockSpec((2, d), lambda i: (0, 0)),
            pl.BlockSpec((d, ff), lambda i: (0, 0)),
            pl.BlockSpec((d, ff), lambda i: (0, 0)),
            pl.BlockSpec((ff, d), lambda i: (0, 0)),
            pl.BlockSpec(memory_space=pl.ANY),
        ],
        out_specs=pl.BlockSpec((t, d), lambda i: (i, 0)),
        out_shape=jax.ShapeDtypeStruct((n, d), F32),
        scratch_shapes=[
            pltpu.SMEM((2 * TOP_K * t,), I32),
            pltpu.VMEM((2, TOP_K, t, c), U32),
            pltpu.SemaphoreType.DMA,
            pltpu.SemaphoreType.DMA((2,)),
        ],
        compiler_params=_params("arbitrary"),
        name="combine",
    )(slot_tiles, h1, w, mod, g23, sg, su, sd, ys)


def _rope_tables(n_ctx, seq):
    nf = HEAD_DIM // 4
    inv = ROPE_THETA ** (-jnp.arange(nf, dtype=F32) / nf)
    pos = jnp.arange(seq, dtype=I32)
    row_ang = (pos // GRID_W).astype(F32)[:, None] * inv
    col_ang = (pos % GRID_W).astype(F32)[:, None] * inv
    cos = jnp.concatenate([jnp.cos(row_ang)] * 2 + [jnp.cos(col_ang)] * 2, axis=1)
    sin = jnp.concatenate([-jnp.sin(row_ang), jnp.sin(row_ang), -jnp.sin(col_ang), jnp.sin(col_ang)], axis=1)
    cos = jnp.concatenate([jnp.ones((n_ctx, HEAD_DIM), F32), cos], axis=0)
    sin = jnp.concatenate([jnp.zeros((n_ctx, HEAD_DIM), F32), sin], axis=0)
    return cos, sin


def _split_router(router):
    hi = router.astype(BF16)
    lo = (router - hi.astype(F32)).astype(BF16)
    return jnp.concatenate([hi, lo], axis=1)


def _moe(layer, h1, fp, logits, mod, g23, bias, wg, wu, wd, sg, su, sd, n_ctx):
    n = h1.shape[0]
    t = ROUTE_TILE
    tm = EXPERT_TILE
    top_e, w, rank, cnt = _route(logits, bias.reshape(N_EXPERTS, 1))
    counts = cnt[:, 0].astype(I32)
    padded = (counts + tm - 1) // tm * tm
    pad_end = jnp.cumsum(padded)
    pad_start = pad_end - padded
    onehot = top_e[:, :, None] == jnp.arange(N_EXPERTS, dtype=I32)
    slot = rank + jnp.sum(jnp.where(onehot, pad_start, 0), axis=-1)
    slot_tiles = slot.reshape(TOP_K, n // t, t).transpose(1, 0, 2).reshape(n // t, TOP_K * t)
    cap = n * TOP_K + N_EXPERTS * tm
    nb = cap // tm
    n_tiles = (pad_end[-1] // tm).astype(I32).reshape(1)
    tile_row = jnp.minimum(jnp.arange(nb, dtype=I32), n_tiles[0] - 1) * tm
    block_e = jnp.sum((pad_end[None, :] <= tile_row[:, None]).astype(I32), axis=1)
    block_e = jnp.minimum(block_e, N_EXPERTS - 1).astype(I32)
    meta = jnp.stack([pad_start, counts, padded]).astype(I32)
    xs = _dispatch(meta, n_tiles, slot_tiles, fp, cap)
    ys = _experts(layer, block_e, n_tiles, xs, wg, wu, wd)
    return _combine(slot_tiles, h1, w, mod, g23, sg, su, sd, ys, n_ctx)


def kernel(x, c, ctx, c_ctx, ada_w, ada_b, norm_g, ev_w_in, ev_w_out, ev_q_norm, ev_k_norm, ev_dw_w, ev_dw_b,
           ev_ln_g, ev_ln_b, od_w_in, od_w_out, od_lambda, od_subln_g, moe_router, moe_bias, moe_w_gate,
           moe_w_up, moe_w_down, sh_w_gate, sh_w_up, sh_w_down):
    b, seq, d = x.shape
    n_ctx = ctx.shape[1]
    assert b == 1 and n_ctx % ROW_TILE == 0 and seq % ROW_TILE == 0 and seq % GRID_W == 0
    depth = ada_w.shape[0]
    assert depth == 2
    cos, sin = _rope_tables(n_ctx, seq)
    cvec = jnp.stack([c[0], c_ctx], axis=1)
    mod_all = _ada(cvec, ada_w, ada_b).reshape(depth, 2, 6, d)
    h = jnp.concatenate([ctx[0], x[0]], axis=0)
    qscale = math.log2(math.e) * HEAD_DIM ** -0.5
    c_conv = ev_dw_w.shape[-1]
    wg, wu, wd = moe_w_gate, moe_w_up, moe_w_down
    sg, su, sd = sh_w_gate.astype(BF16), sh_w_up.astype(BF16), sh_w_down.astype(BF16)

    mod = mod_all[0]
    gains = norm_g[0]
    nq = A_HEADS * HEAD_DIM
    nkv = 2 * A_KV_HEADS * HEAD_DIM
    w_in = jnp.concatenate([ev_w_in[0][:, :nq], ev_w_in[0][:, nq + nkv:], ev_w_in[0][:, nq:nq + nkv]], axis=1)
    qkvu = _modmm(h, mod, gains[0:1], w_in.astype(BF16), n_ctx, col_tile=w_in.shape[1] // 2)
    u_col0 = nq
    k_col0 = nq + 2 * c_conv
    v_col0 = k_col0 + A_KV_HEADS * HEAD_DIM
    q, qmax = _prep(qkvu, 0, 0, A_HEADS, cos, sin, ev_q_norm[0].reshape(1, HEAD_DIM), True, qscale, False)
    kt, kmax = _prep(qkvu, 0, k_col0, A_KV_HEADS, cos, sin, ev_k_norm[0].reshape(1, HEAD_DIM), True, 1.0, True)
    o_attn = _gqa_attention(q, qmax, kt, kmax, qkvu, v_col0, n_ctx)
    o_conv = _conformer_conv(qkvu, u_col0, c_conv, ev_dw_w[0], ev_dw_b[0].reshape(1, -1),
                             ev_ln_g[0].reshape(1, -1), ev_ln_b[0].reshape(1, -1), n_ctx)
    h1, fp, logits = _post([o_attn, o_conv], ev_w_out[0].astype(BF16), h, 0, mod, gains[1:3],
                           _split_router(moe_router[0]), n_ctx)
    h = _moe(0, h1, fp, logits, mod, gains[2:4], moe_bias[0], wg, wu, wd, sg[0], su[0], sd[0], n_ctx)

    mod = mod_all[1]
    gains = norm_g[1]
    lam_init = 0.8 - 0.6 * math.exp(-0.3 * 1)
    lp = od_lambda[0].astype(F32)
    lam = (jnp.exp(jnp.sum(lp[0] * lp[1])) - jnp.exp(jnp.sum(lp[2] * lp[3])) + lam_init).reshape(1)
    qkv = _modmm(h, mod, gains[0:1], od_w_in[0].astype(BF16), n_ctx, col_tile=od_w_in.shape[2] // 4)
    n_qk = 2 * DIFF_HEADS
    ones = jnp.ones((1, HEAD_DIM), F32)
    q, qmax = _prep(qkv, n_ctx, 0, n_qk, cos, sin, ones, False, qscale, False)
    kt, kmax = _prep(qkv, 0, n_qk * HEAD_DIM, n_qk, cos, sin, ones, False, 1.0, True)
    o_attn = _diff_attention(lam, q, qmax, kt, kmax, qkv, 2 * n_qk * HEAD_DIM, od_subln_g[0].reshape(1, -1),
                             1.0 - lam_init, n_ctx)
    h1, fp, logits = _post([o_attn], od_w_out[0].astype(BF16), h, n_ctx, mod, gains[1:3],
                           _split_router(moe_router[1]), n_ctx)
    h = _moe(1, h1, fp, logits, mod, gains[2:4], moe_bias[1], wg, wu, wd, sg[1], su[1], sd[1], 0)
    return h[None]
```

```python
import functools
import math

import jax
import jax.numpy as jnp
from jax import lax
from jax.experimental import pallas as pl
from jax.experimental.pallas import tpu as pltpu

F32 = jnp.float32
BF16 = jnp.bfloat16
U32 = jnp.uint32
I32 = jnp.int32

GRID_W = 64
HEAD_DIM = 128
ROPE_THETA = 10000.0
EPS = 1e-6
A_HEADS = 8
A_KV_HEADS = 2
CONV_W = 31
DIFF_HEADS = 8
N_EXPERTS = 64
TOP_K = 8
N_GROUPS = 8
TOPK_GROUPS = 4
ROUTED_SCALE = 2.5

LANES = 128
SUBLANES = 8
VMEM_LIMIT_BYTES = 56 * 1024 * 1024

ROW_TILE = 256
KV_CHUNK = 256
LAT_SUB = 8
Q_TILE_GQA = 256
Q_TILE_DIFF = 1024
ROUTE_TILE = 256
EXPERT_TILE = 256
CONV_HALO = 16
NEG_BIG = -1e30
FAST_BOUND_MAX = 60.0


def _params(*sem):
    return pltpu.CompilerParams(dimension_semantics=sem, vmem_limit_bytes=VMEM_LIMIT_BYTES)


def _sigmoid(x):
    return 1.0 / (1.0 + jnp.exp(-x))


def _rms(x, g):
    return x * lax.rsqrt(jnp.mean(x * x, axis=-1, keepdims=True) + EPS) * g


def _modulate(x, g, shift, scale):
    return _rms(x, g) * (1.0 + scale) + shift


def _pack_halves(x):
    c = x.shape[1] // 2
    lo = lax.bitcast_convert_type(x[:, :c].astype(BF16).astype(F32), U32)
    hi = lax.bitcast_convert_type(x[:, c:].astype(BF16).astype(F32), U32)
    return (hi & jnp.uint32(0xFFFF0000)) | (lo >> 16)


def _unpack_halves(w):
    lo = lax.bitcast_convert_type(w << 16, F32)
    hi = lax.bitcast_convert_type(w & jnp.uint32(0xFFFF0000), F32)
    return lo, hi


ADA_COLS = 512
ADA_KCHUNK = 64


def _ada_kernel(cv_ref, w_ref, b_ref, o_ref, s_ref):
    d = cv_ref.shape[0]
    tn = o_ref.shape[-1]

    @pl.when(jnp.logical_and(pl.program_id(0) == 0, pl.program_id(1) == 0))
    def _():
        cv = cv_ref[...]
        s = cv * _sigmoid(cv)
        s_ref[0] = jnp.broadcast_to(s[:, 0:1], (d, LANES))
        s_ref[1] = jnp.broadcast_to(s[:, 1:2], (d, LANES))

    def body(k, acc):
        a0, a1 = acc
        r0 = pl.multiple_of(k * ADA_KCHUNK, ADA_KCHUNK)
        w = w_ref[0, pl.ds(r0, ADA_KCHUNK), :]
        s0 = _lane_tile(s_ref[0, pl.ds(r0, ADA_KCHUNK), :], tn // LANES)
        s1 = _lane_tile(s_ref[1, pl.ds(r0, ADA_KCHUNK), :], tn // LANES)
        p0 = (w * s0).reshape(ADA_KCHUNK // SUBLANES, SUBLANES, tn)
        p1 = (w * s1).reshape(ADA_KCHUNK // SUBLANES, SUBLANES, tn)
        return a0 + jnp.sum(p0, axis=0), a1 + jnp.sum(p1, axis=0)

    z = jnp.zeros((SUBLANES, tn), F32)
    a0, a1 = lax.fori_loop(0, d // ADA_KCHUNK, body, (z, z))
    b = b_ref[0]
    o_ref[0, 0:1, :] = jnp.sum(a0, axis=0, keepdims=True) + b
    o_ref[0, 1:2, :] = jnp.sum(a1, axis=0, keepdims=True) + b


def _ada(cvec, ada_w, ada_b):
    depth, d, n = ada_w.shape
    return pl.pallas_call(
        _ada_kernel,
        grid=(depth, n // ADA_COLS),
        in_specs=[
            pl.BlockSpec((d, 2), lambda l, j: (0, 0)),
            pl.BlockSpec((1, d, ADA_COLS), lambda l, j: (l, 0, j)),
            pl.BlockSpec((1, 1, ADA_COLS), lambda l, j: (l, 0, j)),
        ],
        out_specs=pl.BlockSpec((1, 2, ADA_COLS), lambda l, j: (l, 0, j)),
        out_shape=jax.ShapeDtypeStruct((depth, 2, n), F32),
        scratch_shapes=[pltpu.VMEM((2, d, LANES), F32)],
        compiler_params=_params("arbitrary", "arbitrary"),
        name="ada",
    )(cvec, ada_w, ada_b.reshape(depth, 1, n))


def _modmm_kernel(h_ref, mod_ref, g_ref, w_ref, o_ref):
    a = _modulate(h_ref[...], g_ref[...], mod_ref[0, 0:1, :], mod_ref[0, 1:2, :])
    o_ref[...] = jnp.dot(a.astype(BF16), w_ref[...], preferred_element_type=F32).astype(o_ref.dtype)


def _modmm(h, mod, g, w, n_ctx, col_tile):
    r, d = h.shape
    n = w.shape[1]
    ctx_tiles = n_ctx // ROW_TILE
    return pl.pallas_call(
        _modmm_kernel,
        grid=(n // col_tile, r // ROW_TILE),
        in_specs=[
            pl.BlockSpec((ROW_TILE, d), lambda j, i: (i, 0)),
            pl.BlockSpec((1, 6, d), lambda j, i: (jnp.where(i < ctx_tiles, 1, 0), 0, 0)),
            pl.BlockSpec((1, d), lambda j, i: (0, 0)),
            pl.BlockSpec((d, col_tile), lambda j, i: (0, j)),
        ],
        out_specs=pl.BlockSpec((ROW_TILE, col_tile), lambda j, i: (i, j)),
        out_shape=jax.ShapeDtypeStruct((r, n), BF16),
        compiler_params=_params("arbitrary", "arbitrary"),
        name="modmm",
    )(h, mod, g, w)


def _rope(x, c, s):
    return x * c + pltpu.roll(x, HEAD_DIM // 2, axis=1) * s


def _prep_kernel(x_ref, c_ref, s_ref, g_ref, o_ref, st_ref, *, n_heads, norm, scale, transpose):
    c = c_ref[...]
    s = s_ref[...]
    stats = []
    for h in range(n_heads):
        x = x_ref[:, h * HEAD_DIM:(h + 1) * HEAD_DIM].astype(F32)
        if norm:
            x = _rms(x, g_ref[...])
        y = (_rope(x, c, s) * scale).astype(BF16).astype(F32)
        sq = jnp.max(jnp.sum(y * y, axis=-1, keepdims=True), axis=0, keepdims=True)
        stats.append(jnp.broadcast_to(sq, (1, LANES)))
        if transpose:
            o_ref[h, 0] = y.T.astype(o_ref.dtype)
        else:
            o_ref[:, h * HEAD_DIM:(h + 1) * HEAD_DIM] = y.astype(o_ref.dtype)
    st_ref[0] = jnp.concatenate(stats, axis=0)


def _prep(x, row0, col0, n_heads, cos, sin, g, norm, scale, transpose):
    r = x.shape[0] - row0
    width = n_heads * HEAD_DIM
    cb = col0 // width
    tr = KV_CHUNK
    nt = r // tr
    rb = row0 // tr
    if transpose:
        o_spec = pl.BlockSpec((n_heads, 1, HEAD_DIM, tr), lambda i: (0, i, 0, 0))
        o_shape = jax.ShapeDtypeStruct((n_heads, nt, HEAD_DIM, tr), BF16)
    else:
        o_spec = pl.BlockSpec((tr, width), lambda i: (i, 0))
        o_shape = jax.ShapeDtypeStruct((r, width), BF16)
    out, stats = pl.pallas_call(
        functools.partial(_prep_kernel, n_heads=n_heads, norm=norm, scale=scale, transpose=transpose),
        grid=(nt,),
        in_specs=[
            pl.BlockSpec((tr, width), lambda i: (i + rb, cb)),
            pl.BlockSpec((tr, HEAD_DIM), lambda i: (i + rb, 0)),
            pl.BlockSpec((tr, HEAD_DIM), lambda i: (i + rb, 0)),
            pl.BlockSpec((1, HEAD_DIM), lambda i: (0, 0)),
        ],
        out_specs=[o_spec, pl.BlockSpec((1, n_heads, LANES), lambda i: (i, 0, 0))],
        out_shape=[o_shape, jax.ShapeDtypeStruct((nt, n_heads, LANES), F32)],
        compiler_params=_params("arbitrary"),
        name="prep_k" if transpose else "prep_q",
    )(x, cos, sin, g)
    return out, jnp.sqrt(jnp.max(stats[:, :, 0], axis=0))


def _lane_tile(x, n):
    return x if n == 1 else jnp.concatenate([x] * n, axis=1)


def _lane_fold(p):
    out = p[:, 0:LANES]
    for j in range(1, p.shape[1] // LANES):
        out = out + p[:, j * LANES:(j + 1) * LANES]
    return out


def _softmax_step(q, kt, v, m_ref, l_ref, acc_ref, online):
    s = jnp.dot(q, kt, preferred_element_type=F32)
    tiles = s.shape[1] // LANES
    if online:
        m_prev = m_ref[...]
        m_new = jnp.maximum(m_prev, jnp.max(s, axis=-1, keepdims=True))
        alpha = jnp.exp2(m_prev - m_new)
        p = jnp.exp2(s - _lane_tile(m_new, tiles))
        l_ref[...] = alpha * l_ref[...] + _lane_fold(p)
        acc_ref[...] = (_lane_tile(alpha, acc_ref.shape[1] // LANES) * acc_ref[...]
                        + jnp.dot(p.astype(BF16), v, preferred_element_type=F32))
        m_ref[...] = m_new
    else:
        p = jnp.exp2(s - _lane_tile(m_ref[...], tiles))
        l_ref[...] += _lane_fold(p)
        acc_ref[...] += jnp.dot(p.astype(BF16), v, preferred_element_type=F32)


def _scan_keys(kt_ref, n_k, v_ref, n_ctx, n_lat, see_latent, step):
    for cc in range(n_ctx // KV_CHUNK):
        step([kt_ref[j, cc] for j in range(n_k)], v_ref[cc * KV_CHUNK:(cc + 1) * KV_CHUNK, :])
    ctx_chunks = n_ctx // KV_CHUNK

    def latent():
        def body(c, carry):
            base = ctx_chunks + c * LAT_SUB
            kts = [jnp.concatenate([kt_ref[j, base + u] for u in range(LAT_SUB)], axis=1) for j in range(n_k)]
            r0 = pl.multiple_of(n_ctx + c * (LAT_SUB * KV_CHUNK), KV_CHUNK)
            step(kts, v_ref[pl.ds(r0, LAT_SUB * KV_CHUNK), :])
            return carry

        lax.fori_loop(0, n_lat, body, 0)

    if see_latent is None:
        latent()
    else:
        pl.when(see_latent)(latent)


def _row_bound(q, kmax):
    qf = q.astype(F32)
    return jnp.sqrt(jnp.sum(qf * qf, axis=-1, keepdims=True)) * kmax


def _gqa_kernel(fast_ref, kmax_ref, q_ref, kt_ref, v_ref, o_ref, qs_ref, m_ref, l_ref, acc_ref,
                *, ctx_tiles, n_ctx, n_lat):
    g = pl.program_id(0)
    tq = q_ref.shape[0]
    group = q_ref.shape[1] // HEAD_DIM
    for u in range(group):
        qs_ref[u * tq:(u + 1) * tq, :] = q_ref[:, u * HEAD_DIM:(u + 1) * HEAD_DIM]
    l_ref[...] = jnp.zeros(l_ref.shape, F32)
    acc_ref[...] = jnp.zeros(acc_ref.shape, F32)
    see_latent = pl.program_id(1) >= ctx_tiles
    fast = fast_ref[g] == 1

    def scan(online):
        def step(kts, v):
            _softmax_step(qs_ref[...], kts[0], v, m_ref, l_ref, acc_ref, online)

        _scan_keys(kt_ref, 1, v_ref, n_ctx, n_lat, see_latent, step)

    @pl.when(fast)
    def _():
        m_ref[...] = jnp.broadcast_to(_row_bound(qs_ref[...], kmax_ref[g]), m_ref.shape)
        scan(False)

    @pl.when(jnp.logical_not(fast))
    def _():
        m_ref[...] = jnp.full(m_ref.shape, NEG_BIG, F32)
        scan(True)

    o = acc_ref[...] / jnp.sum(l_ref[...], axis=-1, keepdims=True)
    for u in range(group):
        o_ref[:, u * HEAD_DIM:(u + 1) * HEAD_DIM] = o[u * tq:(u + 1) * tq, :].astype(o_ref.dtype)


def _kv_chunks(r, n_ctx):
    n_lat_chunks = (r - n_ctx) // KV_CHUNK
    assert n_ctx % KV_CHUNK == 0 and n_lat_chunks % LAT_SUB == 0
    return n_lat_chunks // LAT_SUB


def _gqa_attention(q, qmax, kt, kmax, qkvu, v_col0, n_ctx):
    r = q.shape[0]
    group = A_HEADS // A_KV_HEADS
    tq = Q_TILE_GQA
    nc = r // KV_CHUNK
    vb = v_col0 // HEAD_DIM
    fast = (jnp.max(qmax.reshape(A_KV_HEADS, group), axis=1) * kmax <= FAST_BOUND_MAX).astype(I32)
    kern = functools.partial(_gqa_kernel, ctx_tiles=n_ctx // tq, n_ctx=n_ctx, n_lat=_kv_chunks(r, n_ctx))
    smem = pl.BlockSpec(memory_space=pltpu.SMEM)
    return pl.pallas_call(
        kern,
        grid=(A_KV_HEADS, r // tq),
        in_specs=[
            smem, smem,
            pl.BlockSpec((tq, group * HEAD_DIM), lambda g, i: (i, g)),
            pl.BlockSpec((1, nc, HEAD_DIM, KV_CHUNK), lambda g, i: (g, 0, 0, 0)),
            pl.BlockSpec((r, HEAD_DIM), lambda g, i: (0, vb + g)),
        ],
        out_specs=pl.BlockSpec((tq, group * HEAD_DIM), lambda g, i: (i, g)),
        out_shape=jax.ShapeDtypeStruct((r, A_HEADS * HEAD_DIM), BF16),
        scratch_shapes=[
            pltpu.VMEM((group * tq, HEAD_DIM), BF16),
            pltpu.VMEM((group * tq, LANES), F32),
            pltpu.VMEM((group * tq, LANES), F32),
            pltpu.VMEM((group * tq, HEAD_DIM), F32),
        ],
        compiler_params=_params("arbitrary", "arbitrary"),
        name="gqa_attn",
    )(fast, kmax, q, kt, qkvu)


def _diff_kernel(lam_ref, fast_ref, kmax_ref, q_ref, kt_ref, v_ref, g_ref, o_ref, m0, l0, a0, m1, l1, a1,
                 *, out_scale, n_ctx, n_lat):
    h = pl.program_id(0)
    comps = ((m0, l0, a0), (m1, l1, a1))
    for _, l_ref, a_ref in comps:
        l_ref[...] = jnp.zeros(l_ref.shape, F32)
        a_ref[...] = jnp.zeros(a_ref.shape, F32)
    fast = fast_ref[h] == 1

    def q_comp(c):
        return q_ref[:, c * HEAD_DIM:(c + 1) * HEAD_DIM]

    def scan(online):
        def step(kts, v):
            for c, (m_ref, l_ref, a_ref) in enumerate(comps):
                _softmax_step(q_comp(c), kts[c], v, m_ref, l_ref, a_ref, online)

        _scan_keys(kt_ref, 2, v_ref, n_ctx, n_lat, None, step)

    @pl.when(fast)
    def _():
        for c, (m_ref, _, _) in enumerate(comps):
            m_ref[...] = jnp.broadcast_to(_row_bound(q_comp(c), kmax_ref[2 * h + c]), m_ref.shape)
        scan(False)

    @pl.when(jnp.logical_not(fast))
    def _():
        for m_ref, _, _ in comps:
            m_ref[...] = jnp.full(m_ref.shape, NEG_BIG, F32)
        scan(True)

    o = (a0[...] / jnp.sum(l0[...], axis=-1, keepdims=True)
         - lam_ref[0] * (a1[...] / jnp.sum(l1[...], axis=-1, keepdims=True)))
    o_ref[...] = (_rms(o, g_ref[...]) * out_scale).astype(o_ref.dtype)


def _diff_attention(lam, q, qmax, kt, kmax, qkv, v_col0, subln_g, out_scale, n_ctx):
    r = qkv.shape[0]
    seq = q.shape[0]
    tq = math.gcd(Q_TILE_DIFF, seq)
    nc = r // KV_CHUNK
    dv = 2 * HEAD_DIM
    vb = v_col0 // dv
    fast = (jnp.max((qmax * kmax).reshape(DIFF_HEADS, 2), axis=1) <= FAST_BOUND_MAX).astype(I32)
    kern = functools.partial(_diff_kernel, out_scale=out_scale, n_ctx=n_ctx, n_lat=_kv_chunks(r, n_ctx))
    stat = pltpu.VMEM((tq, LANES), F32)
    acc = pltpu.VMEM((tq, dv), F32)
    smem = pl.BlockSpec(memory_space=pltpu.SMEM)
    return pl.pallas_call(
        kern,
        grid=(DIFF_HEADS, seq // tq),
        in_specs=[
            smem, smem, smem,
            pl.BlockSpec((tq, dv), lambda h, i: (i, h)),
            pl.BlockSpec((2, nc, HEAD_DIM, KV_CHUNK), lambda h, i: (h, 0, 0, 0), pipeline_mode=pl.Buffered(1)),
            pl.BlockSpec((r, dv), lambda h, i: (0, vb + h), pipeline_mode=pl.Buffered(1)),
            pl.BlockSpec((1, dv), lambda h, i: (0, 0)),
        ],
        out_specs=pl.BlockSpec((tq, dv), lambda h, i: (i, h)),
        out_shape=jax.ShapeDtypeStruct((seq, DIFF_HEADS * dv), BF16),
        scratch_shapes=[stat, stat, acc, stat, stat, acc],
        compiler_params=_params("arbitrary", "arbitrary"),
        name="diff_attn",
    )(lam, fast, kmax, q, kt, qkv, subln_g)


def _conv_kernel(ap_ref, gp_ref, ac_ref, gc_ref, an_ref, gn_ref, w_ref, b_ref, lg_ref, lb_ref, o_ref, h_ref, sh_ref,
                 *, seq_starts, seq_ends):
    i = pl.program_id(0)
    tr = ac_ref.shape[0]
    is_start = functools.reduce(jnp.logical_or, [i == s for s in seq_starts])
    is_end = functools.reduce(jnp.logical_or, [i == e for e in seq_ends])

    def glu(a_ref, g_ref):
        return a_ref[...].astype(F32) * _sigmoid(g_ref[...].astype(F32))

    h_ref[0:CONV_HALO, :] = jnp.where(is_start, 0.0, glu(ap_ref, gp_ref))
    h_ref[CONV_HALO:CONV_HALO + tr, :] = glu(ac_ref, gc_ref)
    h_ref[CONV_HALO + tr:, :] = jnp.where(is_end, 0.0, glu(an_ref, gn_ref))
    rows = sh_ref.shape[1]
    for b in range(1, SUBLANES):
        sh_ref[b - 1] = h_ref[b:b + rows, :]
    acc = jnp.zeros((tr, ac_ref.shape[1]), F32)
    base = CONV_HALO - CONV_W // 2
    for j in range(CONV_W):
        a, b = divmod(base + j, SUBLANES)
        r0 = a * SUBLANES
        src = h_ref[r0:r0 + tr, :] if b == 0 else sh_ref[b - 1, r0:r0 + tr, :]
        acc = acc + w_ref[j:j + 1, :] * src
    y = acc + b_ref[...]
    mu = jnp.mean(y, axis=-1, keepdims=True)
    var = jnp.mean(jnp.square(y - mu), axis=-1, keepdims=True)
    z = (y - mu) * lax.rsqrt(var + EPS) * lg_ref[...] + lb_ref[...]
    o_ref[...] = (z * _sigmoid(z)).astype(o_ref.dtype)


CONV_TILE = 128


def _conformer_conv(qkvu, u_col0, c_conv, dw_w, dw_b, ln_g, ln_b, n_ctx):
    r = qkvu.shape[0]
    tr = CONV_TILE
    nt = r // tr
    hb = tr // CONV_HALO
    ab = u_col0 // c_conv
    gb = ab + 1
    last_halo = r // CONV_HALO - 1
    seq_starts = (0, n_ctx // tr)
    seq_ends = (n_ctx // tr - 1, nt - 1)

    def prev(col):
        return pl.BlockSpec((CONV_HALO, c_conv), lambda i: (jnp.maximum(i * hb - 1, 0), col))

    def cur(col):
        return pl.BlockSpec((tr, c_conv), lambda i: (i, col))

    def nxt(col):
        return pl.BlockSpec((CONV_HALO, c_conv), lambda i: (jnp.minimum((i + 1) * hb, last_halo), col))

    vec = pl.BlockSpec((1, c_conv), lambda i: (0, 0))
    return pl.pallas_call(
        functools.partial(_conv_kernel, seq_starts=seq_starts, seq_ends=seq_ends),
        grid=(nt,),
        in_specs=[prev(ab), prev(gb), cur(ab), cur(gb), nxt(ab), nxt(gb),
                  pl.BlockSpec((CONV_W, c_conv), lambda i: (0, 0)), vec, vec, vec],
        out_specs=pl.BlockSpec((tr, c_conv), lambda i: (i, 0)),
        out_shape=jax.ShapeDtypeStruct((r, c_conv), BF16),
        scratch_shapes=[pltpu.VMEM((tr + 2 * CONV_HALO, c_conv), F32),
                        pltpu.VMEM((SUBLANES - 1, tr + 2 * CONV_HALO - SUBLANES, c_conv), F32)],
        compiler_params=_params("arbitrary"),
        name="conformer_conv",
    )(qkvu, qkvu, qkvu, qkvu, qkvu, qkvu, dw_w, dw_b, ln_g, ln_b)


def _post_kernel(*refs, n_in):
    o_refs = refs[:n_in]
    w_ref, h_ref, mod_ref, g_ref, rr_ref, h1_ref, fp_ref, lg_ref = refs[n_in:]
    k0 = 0
    out = None
    for o_in in o_refs:
        kw = o_in.shape[1]
        part = jnp.dot(o_in[...], w_ref[k0:k0 + kw, :], preferred_element_type=F32)
        out = part if out is None else out + part
        k0 += kw
    h1 = h_ref[...] + mod_ref[0, 2:3, :] * _rms(out, g_ref[0:1, :])
    h1_ref[...] = h1
    f = _modulate(h1, g_ref[1:2, :], mod_ref[0, 3:4, :], mod_ref[0, 4:5, :])
    fp_ref[...] = _pack_halves(f)
    f_hi = f.astype(BF16)
    f_lo = (f - f_hi.astype(F32)).astype(BF16)
    l2 = (jnp.dot(f_hi, rr_ref[...], preferred_element_type=F32)
          + jnp.dot(f_lo, rr_ref[...], preferred_element_type=F32))
    lg_ref[...] = l2 + pltpu.roll(l2, N_EXPERTS, axis=1)


def _post(o_parts, w_out, h, row_off, mod, g12, rr, n_ctx):
    n = o_parts[0].shape[0]
    d = h.shape[1]
    tm = ROW_TILE
    off = row_off // tm
    ctx_tiles = (n_ctx - row_off) // tm
    in_specs = [pl.BlockSpec((tm, o.shape[1]), lambda i: (i, 0)) for o in o_parts]
    in_specs += [
        pl.BlockSpec((d, d), lambda i: (0, 0)),
        pl.BlockSpec((tm, d), lambda i: (i + off, 0)),
        pl.BlockSpec((1, 6, d), lambda i: (jnp.where(i < ctx_tiles, 1, 0), 0, 0)),
        pl.BlockSpec((2, d), lambda i: (0, 0)),
        pl.BlockSpec((d, 2 * N_EXPERTS), lambda i: (0, 0)),
    ]
    return pl.pallas_call(
        functools.partial(_post_kernel, n_in=len(o_parts)),
        grid=(n // tm,),
        in_specs=in_specs,
        out_specs=[
            pl.BlockSpec((tm, d), lambda i: (i, 0)),
            pl.BlockSpec((tm, d // 2), lambda i: (i, 0)),
            pl.BlockSpec((tm, 2 * N_EXPERTS), lambda i: (i, 0)),
        ],
        out_shape=[
            jax.ShapeDtypeStruct((n, d), F32),
            jax.ShapeDtypeStruct((n, d // 2), U32),
            jax.ShapeDtypeStruct((n, 2 * N_EXPERTS), F32),
        ],
        compiler_params=_params("arbitrary"),
        name="post_mixer",
    )(*o_parts, w_out, h, mod, g12, rr)


def _route_kernel(lg_ref, bias_ref, e_ref, w_ref, rank_ref, cnt_ref):
    t = lg_ref.shape[0]
    gsz = N_EXPERTS // N_GROUPS

    @pl.when(pl.program_id(0) == 0)
    def _():
        cnt_ref[...] = jnp.zeros(cnt_ref.shape, F32)

    logits = lg_ref[...].T[0:N_EXPERTS, :]
    scores = _sigmoid(logits)
    choice = scores + bias_ref[...]
    sub = lax.broadcasted_iota(I32, (gsz, t), 0).astype(F32)
    grp_rows = []
    for g in range(N_GROUPS):
        cg = choice[g * gsz:(g + 1) * gsz, :]
        m1 = jnp.max(cg, axis=0, keepdims=True)
        i1 = jnp.min(jnp.where(cg == m1, sub, float(gsz)), axis=0, keepdims=True)
        m2 = jnp.max(jnp.where(sub == i1, -jnp.inf, cg), axis=0, keepdims=True)
        grp_rows.append(m1 + m2)
    grp = jnp.concatenate(grp_rows, axis=0)
    gidx = lax.broadcasted_iota(I32, (N_GROUPS, t), 0).astype(F32)
    gsel = jnp.zeros((N_GROUPS, t), F32)
    for _ in range(TOPK_GROUPS):
        gm = jnp.max(grp, axis=0, keepdims=True)
        gi = jnp.min(jnp.where(grp == gm, gidx, float(N_GROUPS)), axis=0, keepdims=True)
        hit = gidx == gi
        gsel = jnp.where(hit, 1.0, gsel)
        grp = jnp.where(hit, -jnp.inf, grp)
    eidx = lax.broadcasted_iota(I32, (N_EXPERTS, t), 0).astype(F32)
    mc_rows = []
    for g in range(N_GROUPS):
        mc_rows.append(jnp.where(gsel[g:g + 1, :] > 0.5, choice[g * gsz:(g + 1) * gsz, :], -jnp.inf))
    mc = jnp.concatenate(mc_rows, axis=0)
    sel_all = jnp.zeros((N_EXPERTS, t), F32)
    e_rows, w_rows = [], []
    for _ in range(TOP_K):
        mx = jnp.max(mc, axis=0, keepdims=True)
        ei = jnp.min(jnp.where(mc == mx, eidx, float(N_EXPERTS)), axis=0, keepdims=True)
        hit = eidx == ei
        e_rows.append(ei)
        w_rows.append(jnp.sum(jnp.where(hit, scores, 0.0), axis=0, keepdims=True))
        sel_all = jnp.where(hit, 1.0, sel_all)
        mc = jnp.where(hit, -jnp.inf, mc)
    e_all = jnp.concatenate(e_rows, axis=0)
    w_all = jnp.concatenate(w_rows, axis=0)
    w_ref[...] = w_all / jnp.sum(w_all, axis=0, keepdims=True) * ROUTED_SCALE
    e_ref[...] = e_all.astype(I32)
    src = lax.broadcasted_iota(I32, (t, t), 0)
    dst = lax.broadcasted_iota(I32, (t, t), 1)
    upper = jnp.where(src < dst, 1.0, 0.0).astype(BF16)
    pos = jnp.dot(sel_all.astype(BF16), upper, preferred_element_type=F32) + cnt_ref[:, 0:1]
    r_rows = [jnp.sum(jnp.where(eidx == e_rows[k], pos, 0.0), axis=0, keepdims=True) for k in range(TOP_K)]
    rank_ref[...] = jnp.concatenate(r_rows, axis=0).astype(I32)
    cnt_ref[...] = cnt_ref[...] + jnp.sum(sel_all, axis=1, keepdims=True)


def _route(logits, bias):
    n = logits.shape[0]
    t = ROUTE_TILE
    row = pl.BlockSpec((TOP_K, t), lambda i: (0, i))
    return pl.pallas_call(
        _route_kernel,
        grid=(n // t,),
        in_specs=[
            pl.BlockSpec((t, 2 * N_EXPERTS), lambda i: (i, 0)),
            pl.BlockSpec((N_EXPERTS, 1), lambda i: (0, 0)),
        ],
        out_specs=[row, row, row, pl.BlockSpec((N_EXPERTS, LANES), lambda i: (0, 0))],
        out_shape=[
            jax.ShapeDtypeStruct((TOP_K, n), I32),
            jax.ShapeDtypeStruct((TOP_K, n), F32),
            jax.ShapeDtypeStruct((TOP_K, n), I32),
            jax.ShapeDtypeStruct((N_EXPERTS, LANES), F32),
        ],
        compiler_params=_params("arbitrary"),
        name="route",
    )(logits, bias)


def _dispatch_kernel(meta_ref, nt_ref, slot_hbm, fp_ref, xs_hbm, slot_smem, zero_ref, sem_idx, sem_rows, sem_zero):
    i = pl.program_id(0)
    t = fp_ref.shape[0]
    idx_copy = pltpu.make_async_copy(slot_hbm.at[i], slot_smem, sem_idx)
    idx_copy.start()
    idx_copy.wait()

    def issue(tok, carry):
        for k in range(TOP_K):
            s = slot_smem[k * t + tok]
            pltpu.make_async_copy(fp_ref.at[pl.ds(tok, 1), :], xs_hbm.at[pl.ds(s, 1), :],
                                  sem_rows).start(priority=k % 2)
        return carry

    lax.fori_loop(0, t, issue, 0)
    for _ in range(TOP_K):
        pltpu.make_async_copy(fp_ref, xs_hbm.at[pl.ds(0, t), :], sem_rows).wait()

    @pl.when(i == pl.num_programs(0) - 1)
    def _():
        tm = zero_ref.shape[0]
        n_all = xs_hbm.shape[0] // tm
        zero_ref[...] = jnp.zeros(zero_ref.shape, zero_ref.dtype)

        def pad_row(row):
            return pltpu.make_async_copy(zero_ref.at[pl.ds(0, 1), :], xs_hbm.at[pl.ds(row, 1), :], sem_zero)

        def tail_tile(b):
            return pltpu.make_async_copy(zero_ref, xs_hbm.at[pl.ds(pl.multiple_of(b * tm, tm), tm), :], sem_zero)

        def per_expert(e, total):
            first = meta_ref[0, e] + meta_ref[1, e]
            n_pad = meta_ref[2, e] - meta_ref[1, e]

            def one(r, carry):
                pad_row(first + r).start()
                return carry

            lax.fori_loop(0, n_pad, one, 0)
            return total + n_pad

        total = lax.fori_loop(0, N_EXPERTS, per_expert, 0)

        def start_tail(b, carry):
            tail_tile(b).start()
            return carry

        lax.fori_loop(nt_ref[0], n_all, start_tail, 0)

        def wait_row(r, carry):
            pad_row(0).wait()
            return carry

        lax.fori_loop(0, total, wait_row, 0)

        def wait_tail(b, carry):
            tail_tile(b).wait()
            return carry

        lax.fori_loop(nt_ref[0], n_all, wait_tail, 0)


def _dispatch(meta, n_tiles, slot_tiles, fp, cap):
    n, c = fp.shape
    t = ROUTE_TILE
    smem = pl.BlockSpec(memory_space=pltpu.SMEM)
    return pl.pallas_call(
        _dispatch_kernel,
        grid=(n // t,),
        in_specs=[
            smem, smem,
            pl.BlockSpec(memory_space=pl.ANY),
            pl.BlockSpec((t, c), lambda i: (i, 0)),
        ],
        out_specs=pl.BlockSpec(memory_space=pl.ANY),
        out_shape=jax.ShapeDtypeStruct((cap, c), U32),
        scratch_shapes=[
            pltpu.SMEM((TOP_K * t,), I32),
            pltpu.VMEM((EXPERT_TILE, c), U32),
            pltpu.SemaphoreType.DMA, pltpu.SemaphoreType.DMA, pltpu.SemaphoreType.DMA,
        ],
        compiler_params=_params("arbitrary"),
        name="dispatch",
    )(meta, n_tiles, slot_tiles, fp)


def _experts_kernel(be_ref, nt_ref, xs_ref, wg_ref, wu_ref, wd_ref, ys_ref):
    del be_ref
    used = pl.program_id(0) < nt_ref[0]

    @pl.when(jnp.logical_not(used))
    def _():
        ys_ref[...] = jnp.zeros(ys_ref.shape, ys_ref.dtype)

    @pl.when(used)
    def _():
        lo, hi = _unpack_halves(xs_ref[...])
        lo = lo.astype(BF16)
        hi = hi.astype(BF16)
        c = lo.shape[1]
        def proj(w_ref):
            return (jnp.dot(lo, w_ref[0, 0, 0:c, :].astype(BF16), preferred_element_type=F32)
                    + jnp.dot(hi, w_ref[0, 0, c:, :].astype(BF16), preferred_element_type=F32))

        g = proj(wg_ref)
        u = proj(wu_ref)
        a = (g * _sigmoid(g) * u).astype(BF16)
        ys_ref[...] = _pack_halves(jnp.dot(a, wd_ref[0, 0].astype(BF16), preferred_element_type=F32))


def _experts(layer, block_e, n_tiles, xs, wg, wu, wd):
    cap, c = xs.shape
    _, e, d, ff = wg.shape
    tm = EXPERT_TILE

    def row_map(b, be, nt):
        return (jnp.minimum(b, nt[0] - 1), 0)

    grid_spec = pltpu.PrefetchScalarGridSpec(
        num_scalar_prefetch=2,
        grid=(cap // tm,),
        in_specs=[
            pl.BlockSpec((tm, c), row_map),
            pl.BlockSpec((1, 1, d, ff), lambda b, be, nt: (layer, be[b], 0, 0)),
            pl.BlockSpec((1, 1, d, ff), lambda b, be, nt: (layer, be[b], 0, 0)),
            pl.BlockSpec((1, 1, ff, d), lambda b, be, nt: (layer, be[b], 0, 0)),
        ],
        out_specs=pl.BlockSpec((tm, c), lambda b, be, nt: (b, 0)),
    )
    return pl.pallas_call(
        _experts_kernel,
        grid_spec=grid_spec,
        out_shape=jax.ShapeDtypeStruct((cap, c), U32),
        compiler_params=_params("arbitrary"),
        name="experts",
    )(block_e, n_tiles, xs, wg, wu, wd)


def _combine_kernel(slot_hbm, h_ref, w_ref, mod_ref, g_ref, sg_ref, su_ref, sd_ref, ys_hbm, o_ref,
                    slot_smem, ybuf, sem_idx, sem_rows):
    i = pl.program_id(0)
    t = h_ref.shape[0]

    def start_gather(tile, buf):
        base = buf * (TOP_K * t)
        idx_copy = pltpu.make_async_copy(slot_hbm.at[tile], slot_smem.at[pl.ds(base, TOP_K * t)], sem_idx)
        idx_copy.start()
        idx_copy.wait()

        def issue(tok, carry):
            for k in range(TOP_K):
                s = slot_smem[base + k * t + tok]
                pltpu.make_async_copy(ys_hbm.at[pl.ds(s, 1), :], ybuf.at[buf, k, pl.ds(tok, 1), :],
                                      sem_rows.at[buf]).start(priority=k % 2)
            return carry

        lax.fori_loop(0, t, issue, 0)

    @pl.when(i == 0)
    def _():
        start_gather(0, 0)

    cur = i % 2
    for buf in (0, 1):
        @pl.when(jnp.logical_and(i + 1 < pl.num_programs(0), cur != buf))
        def _():
            start_gather(i + 1, buf)

    h1 = h_ref[...]
    f = _modulate(h1, g_ref[0:1, :], mod_ref[0, 3:4, :], mod_ref[0, 4:5, :]).astype(BF16)
    sg = jnp.dot(f, sg_ref[...], preferred_element_type=F32)
    su = jnp.dot(f, su_ref[...], preferred_element_type=F32)
    y = jnp.dot((sg * _sigmoid(sg) * su).astype(BF16), sd_ref[...], preferred_element_type=F32)
    for k in range(TOP_K):
        pltpu.make_async_copy(ys_hbm.at[pl.ds(0, t), :], ybuf.at[cur, k], sem_rows.at[cur]).wait()
    c = ybuf.shape[3]
    wt = w_ref[...].T
    y_lo = y[:, 0:c]
    y_hi = y[:, c:]
    for k in range(TOP_K):
        lo, hi = _unpack_halves(ybuf[cur, k])
        wk = wt[:, k:k + 1]
        y_lo = y_lo + wk * lo
        y_hi = y_hi + wk * hi
    y = jnp.concatenate([y_lo, y_hi], axis=1)
    o_ref[...] = h1 + mod_ref[0, 5:6, :] * _rms(y, g_ref[1:2, :])


def _combine(slot_tiles, h1, w, mod, g23, sg, su, sd, ys, n_ctx):
    n, d = h1.shape
    t = ROUTE_TILE
    c = ys.shape[1]
    ff = sg.shape[1]
    ctx_tiles = n_ctx // t
    return pl.pallas_call(
        _combine_kernel,
        grid=(n // t,),
        in_specs=[
            pl.BlockSpec(memory_space=pl.ANY),
            pl.BlockSpec((t, d), lambda i: (i, 0)),
            pl.BlockSpec((TOP_K, t), lambda i: (0, i)),
            pl.BlockSpec((1, 6, d), lambda i: (jnp.where(i < ctx_tiles, 1, 0), 0, 0)),
            pl.BlockSpec((2, d), lambda i: (0, 0)),
            pl.BlockSpec((d, ff), lambda i: (0, 0)),
            pl.BlockSpec((d, ff), lambda i: (0, 0)),
            pl.BlockSpec((ff, d), lambda i: (0, 0)),
            pl.BlockSpec(memory_space=pl.ANY),
        ],
        out_specs=pl.BlockSpec((t, d), lambda i: (i, 0)),
        out_shape=jax.ShapeDtypeStruct((n, d), F32),
        scratch_shapes=[
            pltpu.SMEM((2 * TOP_K * t,), I32),
            pltpu.VMEM((2, TOP_K, t, c), U32),
            pltpu.SemaphoreType.DMA,
            pltpu.SemaphoreType.DMA((2,)),
        ],
        compiler_params=_params("arbitrary"),
        name="combine",
    )(slot_tiles, h1, w, mod, g23, sg, su, sd, ys)


def _rope_tables(n_ctx, seq):
    nf = HEAD_DIM // 4
    rows = seq // GRID_W
    inv = ROPE_THETA ** (-jnp.arange(nf, dtype=F32) / nf)
    row_ang = jnp.arange(rows, dtype=F32)[:, None] * inv
    col_ang = jnp.arange(GRID_W, dtype=F32)[:, None] * inv

    def per_token(row_tab, col_tab):
        return jnp.repeat(row_tab, GRID_W, axis=0), jnp.tile(col_tab, (rows, 1))

    cos_r, cos_c = per_token(jnp.cos(row_ang), jnp.cos(col_ang))
    sin_r, sin_c = per_token(jnp.sin(row_ang), jnp.sin(col_ang))
    cos = jnp.concatenate([cos_r, cos_c, cos_r, cos_c], axis=1)
    sin = jnp.concatenate([-sin_r, -sin_c, sin_r, sin_c], axis=1)
    cos = jnp.concatenate([jnp.ones((n_ctx, HEAD_DIM), F32), cos], axis=0)
    sin = jnp.concatenate([jnp.zeros((n_ctx, HEAD_DIM), F32), sin], axis=0)
    return cos, sin


def _head_lane_order(x):
    q = HEAD_DIM // 4
    y = x.reshape(x.shape[:-1] + (x.shape[-1] // HEAD_DIM, 4, q))
    return y[..., jnp.array([0, 2, 1, 3]), :].reshape(x.shape)


def _split_router(router):
    hi = router.astype(BF16)
    lo = (router - hi.astype(F32)).astype(BF16)
    return jnp.concatenate([hi, lo], axis=1)


def _moe(layer, h1, fp, logits, mod, g23, bias, wg, wu, wd, sg, su, sd, n_ctx):
    n = h1.shape[0]
    t = ROUTE_TILE
    tm = EXPERT_TILE
    top_e, w, rank, cnt = _route(logits, bias.reshape(N_EXPERTS, 1))
    counts = cnt[:, 0].astype(I32)
    padded = (counts + tm - 1) // tm * tm
    pad_end = jnp.cumsum(padded)
    pad_start = pad_end - padded
    onehot = top_e[:, :, None] == jnp.arange(N_EXPERTS, dtype=I32)
    slot = rank + jnp.sum(jnp.where(onehot, pad_start, 0), axis=-1)
    slot_tiles = slot.reshape(TOP_K, n // t, t).transpose(1, 0, 2).reshape(n // t, TOP_K * t)
    cap = n * TOP_K + N_EXPERTS * tm
    nb = cap // tm
    n_tiles = (pad_end[-1] // tm).astype(I32).reshape(1)
    tile_row = jnp.minimum(jnp.arange(nb, dtype=I32), n_tiles[0] - 1) * tm
    block_e = jnp.sum((pad_end[None, :] <= tile_row[:, None]).astype(I32), axis=1)
    block_e = jnp.minimum(block_e, N_EXPERTS - 1).astype(I32)
    meta = jnp.stack([pad_start, counts, padded]).astype(I32)
    xs = _dispatch(meta, n_tiles, slot_tiles, fp, cap)
    ys = _experts(layer, block_e, n_tiles, xs, wg, wu, wd)
    return _combine(slot_tiles, h1, w, mod, g23, sg, su, sd, ys, n_ctx)


def kernel(x, c, ctx, c_ctx, ada_w, ada_b, norm_g, ev_w_in, ev_w_out, ev_q_norm, ev_k_norm, ev_dw_w, ev_dw_b,
           ev_ln_g, ev_ln_b, od_w_in, od_w_out, od_lambda, od_subln_g, moe_router, moe_bias, moe_w_gate,
           moe_w_up, moe_w_down, sh_w_gate, sh_w_up, sh_w_down):
    b, seq, d = x.shape
    n_ctx = ctx.shape[1]
    assert b == 1 and n_ctx % ROW_TILE == 0 and seq % ROW_TILE == 0 and seq % GRID_W == 0
    depth = ada_w.shape[0]
    assert depth == 2
    cos, sin = _rope_tables(n_ctx, seq)
    cvec = jnp.stack([c[0], c_ctx], axis=1)
    mod_all = _ada(cvec, ada_w, ada_b).reshape(depth, 2, 6, d)
    h = jnp.concatenate([ctx[0], x[0]], axis=0)
    qscale = math.log2(math.e) * HEAD_DIM ** -0.5
    c_conv = ev_dw_w.shape[-1]
    wg, wu, wd = moe_w_gate, moe_w_up, moe_w_down
    sg, su, sd = sh_w_gate.astype(BF16), sh_w_up.astype(BF16), sh_w_down.astype(BF16)

    mod = mod_all[0]
    gains = norm_g[0]
    nq = A_HEADS * HEAD_DIM
    nkv = 2 * A_KV_HEADS * HEAD_DIM
    nk = A_KV_HEADS * HEAD_DIM
    w0 = ev_w_in[0]
    w_in = jnp.concatenate([_head_lane_order(w0[:, :nq]), w0[:, nq + nkv:],
                            _head_lane_order(w0[:, nq:nq + nk]), w0[:, nq + nk:nq + nkv]], axis=1)
    qkvu = _modmm(h, mod, gains[0:1], w_in.astype(BF16), n_ctx, col_tile=w_in.shape[1] // 2)
    u_col0 = nq
    k_col0 = nq + 2 * c_conv
    v_col0 = k_col0 + nk
    q_gain = _head_lane_order(ev_q_norm[0].reshape(1, HEAD_DIM))
    k_gain = _head_lane_order(ev_k_norm[0].reshape(1, HEAD_DIM))
    q, qmax = _prep(qkvu, 0, 0, A_HEADS, cos, sin, q_gain, True, qscale, False)
    kt, kmax = _prep(qkvu, 0, k_col0, A_KV_HEADS, cos, sin, k_gain, True, 1.0, True)
    o_attn = _gqa_attention(q, qmax, kt, kmax, qkvu, v_col0, n_ctx)
    o_conv = _conformer_conv(qkvu, u_col0, c_conv, ev_dw_w[0], ev_dw_b[0].reshape(1, -1),
                             ev_ln_g[0].reshape(1, -1), ev_ln_b[0].reshape(1, -1), n_ctx)
    h1, fp, logits = _post([o_attn, o_conv], ev_w_out[0].astype(BF16), h, 0, mod, gains[1:3],
                           _split_router(moe_router[0]), n_ctx)
    h = _moe(0, h1, fp, logits, mod, gains[2:4], moe_bias[0], wg, wu, wd, sg[0], su[0], sd[0], n_ctx)

    mod = mod_all[1]
    gains = norm_g[1]
    lam_init = 0.8 - 0.6 * math.exp(-0.3 * 1)
    lp = od_lambda[0].astype(F32)
    lam = (jnp.exp(jnp.sum(lp[0] * lp[1])) - jnp.exp(jnp.sum(lp[2] * lp[3])) + lam_init).reshape(1)
    n_qk = 2 * DIFF_HEADS
    w1 = od_w_in[0]
    w1 = jnp.concatenate([_head_lane_order(w1[:, :2 * n_qk * HEAD_DIM]), w1[:, 2 * n_qk * HEAD_DIM:]], axis=1)
    qkv = _modmm(h, mod, gains[0:1], w1.astype(BF16), n_ctx, col_tile=w1.shape[1] // 4)
    ones = jnp.ones((1, HEAD_DIM), F32)
    q, qmax = _prep(qkv, n_ctx, 0, n_qk, cos, sin, ones, False, qscale, False)
    kt, kmax = _prep(qkv, 0, n_qk * HEAD_DIM, n_qk, cos, sin, ones, False, 1.0, True)
    o_attn = _diff_attention(lam, q, qmax, kt, kmax, qkv, 2 * n_qk * HEAD_DIM, od_subln_g[0].reshape(1, -1),
                             1.0 - lam_init, n_ctx)
    h1, fp, logits = _post([o_attn], od_w_out[0].astype(BF16), h, n_ctx, mod, gains[1:3],
                           _split_router(moe_router[1]), n_ctx)
    h = _moe(1, h1, fp, logits, mod, gains[2:4], moe_bias[1], wg, wu, wd, sg[1], su[1], sd[1], 0)
    return h[None]
```

```python
import functools
import math

import jax
import jax.numpy as jnp
from jax import lax
from jax.experimental import pallas as pl
from jax.experimental.pallas import tpu as pltpu

F32 = jnp.float32
BF16 = jnp.bfloat16
U32 = jnp.uint32
I32 = jnp.int32

GRID_W = 64
HEAD_DIM = 128
ROPE_THETA = 10000.0
EPS = 1e-6
A_HEADS = 8
A_KV_HEADS = 2
CONV_W = 31
DIFF_HEADS = 8
N_EXPERTS = 64
TOP_K = 8
N_GROUPS = 8
TOPK_GROUPS = 4
ROUTED_SCALE = 2.5

LANES = 128
SUBLANES = 8
VMEM_LIMIT_BYTES = 56 * 1024 * 1024

ROW_TILE = 256
KV_CHUNK = 256
LAT_SUB = 8
Q_TILE_GQA = 256
Q_TILE_DIFF = 1024
ROUTE_TILE = 256
EXPERT_TILE = 256
CONV_HALO = 16
NEG_BIG = -1e30
FAST_BOUND_MAX = 60.0


def _params(*sem):
    return pltpu.CompilerParams(dimension_semantics=sem, vmem_limit_bytes=VMEM_LIMIT_BYTES)


def _sigmoid(x):
    return 1.0 / (1.0 + jnp.exp(-x))


def _rms(x, g):
    return x * lax.rsqrt(jnp.mean(x * x, axis=-1, keepdims=True) + EPS) * g


def _modulate(x, g, shift, scale):
    return _rms(x, g) * (1.0 + scale) + shift


def _pack_halves(x):
    c = x.shape[1] // 2
    lo = lax.bitcast_convert_type(x[:, :c].astype(BF16).astype(F32), U32)
    hi = lax.bitcast_convert_type(x[:, c:].astype(BF16).astype(F32), U32)
    return (hi & jnp.uint32(0xFFFF0000)) | (lo >> 16)


def _unpack_halves(w):
    lo = lax.bitcast_convert_type(w << 16, F32)
    hi = lax.bitcast_convert_type(w & jnp.uint32(0xFFFF0000), F32)
    return lo, hi


ADA_COLS = 512
ADA_KCHUNK = 64


def _ada_kernel(cv_ref, w_ref, b_ref, o_ref, s_ref):
    d = cv_ref.shape[0]
    tn = o_ref.shape[-1]

    @pl.when(jnp.logical_and(pl.program_id(0) == 0, pl.program_id(1) == 0))
    def _():
        cv = cv_ref[...]
        s = cv * _sigmoid(cv)
        s_ref[0] = jnp.broadcast_to(s[:, 0:1], (d, LANES))
        s_ref[1] = jnp.broadcast_to(s[:, 1:2], (d, LANES))

    def body(k, acc):
        a0, a1 = acc
        r0 = pl.multiple_of(k * ADA_KCHUNK, ADA_KCHUNK)
        w = w_ref[0, pl.ds(r0, ADA_KCHUNK), :]
        s0 = _lane_tile(s_ref[0, pl.ds(r0, ADA_KCHUNK), :], tn // LANES)
        s1 = _lane_tile(s_ref[1, pl.ds(r0, ADA_KCHUNK), :], tn // LANES)
        p0 = (w * s0).reshape(ADA_KCHUNK // SUBLANES, SUBLANES, tn)
        p1 = (w * s1).reshape(ADA_KCHUNK // SUBLANES, SUBLANES, tn)
        return a0 + jnp.sum(p0, axis=0), a1 + jnp.sum(p1, axis=0)

    z = jnp.zeros((SUBLANES, tn), F32)
    a0, a1 = lax.fori_loop(0, d // ADA_KCHUNK, body, (z, z))
    b = b_ref[0]
    o_ref[0, 0:1, :] = jnp.sum(a0, axis=0, keepdims=True) + b
    o_ref[0, 1:2, :] = jnp.sum(a1, axis=0, keepdims=True) + b


def _ada(cvec, ada_w, ada_b):
    depth, d, n = ada_w.shape
    return pl.pallas_call(
        _ada_kernel,
        grid=(depth, n // ADA_COLS),
        in_specs=[
            pl.BlockSpec((d, 2), lambda l, j: (0, 0)),
            pl.BlockSpec((1, d, ADA_COLS), lambda l, j: (l, 0, j)),
            pl.BlockSpec((1, 1, ADA_COLS), lambda l, j: (l, 0, j)),
        ],
        out_specs=pl.BlockSpec((1, 2, ADA_COLS), lambda l, j: (l, 0, j)),
        out_shape=jax.ShapeDtypeStruct((depth, 2, n), F32),
        scratch_shapes=[pltpu.VMEM((2, d, LANES), F32)],
        compiler_params=_params("arbitrary", "arbitrary"),
        name="ada",
    )(cvec, ada_w, ada_b.reshape(depth, 1, n))


def _modmm_kernel(h_ref, mod_ref, g_ref, w_ref, o_ref):
    a = _modulate(h_ref[...], g_ref[...], mod_ref[0, 0:1, :], mod_ref[0, 1:2, :])
    o_ref[...] = jnp.dot(a.astype(BF16), w_ref[...], preferred_element_type=F32).astype(o_ref.dtype)


def _modmm(h, mod, g, w, n_ctx, col_tile):
    r, d = h.shape
    n = w.shape[1]
    ctx_tiles = n_ctx // ROW_TILE
    return pl.pallas_call(
        _modmm_kernel,
        grid=(n // col_tile, r // ROW_TILE),
        in_specs=[
            pl.BlockSpec((ROW_TILE, d), lambda j, i: (i, 0)),
            pl.BlockSpec((1, 6, d), lambda j, i: (jnp.where(i < ctx_tiles, 1, 0), 0, 0)),
            pl.BlockSpec((1, d), lambda j, i: (0, 0)),
            pl.BlockSpec((d, col_tile), lambda j, i: (0, j)),
        ],
        out_specs=pl.BlockSpec((ROW_TILE, col_tile), lambda j, i: (i, j)),
        out_shape=jax.ShapeDtypeStruct((r, n), BF16),
        compiler_params=_params("arbitrary", "arbitrary"),
        name="modmm",
    )(h, mod, g, w)


def _rope(x, c, s):
    return x * c + pltpu.roll(x, HEAD_DIM // 2, axis=1) * s


def _prep_kernel(x_ref, c_ref, s_ref, g_ref, o_ref, st_ref, *, n_heads, norm, scale, transpose):
    c = c_ref[...]
    s = s_ref[...]
    stats = []
    for h in range(n_heads):
        x = x_ref[:, h * HEAD_DIM:(h + 1) * HEAD_DIM].astype(F32)
        if norm:
            x = _rms(x, g_ref[...])
        y = (_rope(x, c, s) * scale).astype(BF16).astype(F32)
        sq = jnp.max(jnp.sum(y * y, axis=-1, keepdims=True), axis=0, keepdims=True)
        stats.append(jnp.broadcast_to(sq, (1, LANES)))
        if transpose:
            o_ref[h, 0] = y.T.astype(o_ref.dtype)
        else:
            o_ref[:, h * HEAD_DIM:(h + 1) * HEAD_DIM] = y.astype(o_ref.dtype)
    st_ref[0] = jnp.concatenate(stats, axis=0)


def _prep(x, row0, col0, n_heads, cos, sin, g, norm, scale, transpose):
    r = x.shape[0] - row0
    width = n_heads * HEAD_DIM
    cb = col0 // width
    tr = KV_CHUNK
    nt = r // tr
    rb = row0 // tr
    if transpose:
        o_spec = pl.BlockSpec((n_heads, 1, HEAD_DIM, tr), lambda i: (0, i, 0, 0))
        o_shape = jax.ShapeDtypeStruct((n_heads, nt, HEAD_DIM, tr), BF16)
    else:
        o_spec = pl.BlockSpec((tr, width), lambda i: (i, 0))
        o_shape = jax.ShapeDtypeStruct((r, width), BF16)
    out, stats = pl.pallas_call(
        functools.partial(_prep_kernel, n_heads=n_heads, norm=norm, scale=scale, transpose=transpose),
        grid=(nt,),
        in_specs=[
            pl.BlockSpec((tr, width), lambda i: (i + rb, cb)),
            pl.BlockSpec((tr, HEAD_DIM), lambda i: (i + rb, 0)),
            pl.BlockSpec((tr, HEAD_DIM), lambda i: (i + rb, 0)),
            pl.BlockSpec((1, HEAD_DIM), lambda i: (0, 0)),
        ],
        out_specs=[o_spec, pl.BlockSpec((1, n_heads, LANES), lambda i: (i, 0, 0))],
        out_shape=[o_shape, jax.ShapeDtypeStruct((nt, n_heads, LANES), F32)],
        compiler_params=_params("arbitrary"),
        name="prep_k" if transpose else "prep_q",
    )(x, cos, sin, g)
    return out, jnp.sqrt(jnp.max(stats[:, :, 0], axis=0))


def _lane_tile(x, n):
    return x if n == 1 else jnp.concatenate([x] * n, axis=1)


def _lane_fold(p):
    out = p[:, 0:LANES]
    for j in range(1, p.shape[1] // LANES):
        out = out + p[:, j * LANES:(j + 1) * LANES]
    return out


def _softmax_step(q, kt, v, m_ref, l_ref, acc_ref, online):
    s = jnp.dot(q, kt, preferred_element_type=F32)
    tiles = s.shape[1] // LANES
    if online:
        m_prev = m_ref[...]
        m_new = jnp.maximum(m_prev, jnp.max(s, axis=-1, keepdims=True))
        alpha = jnp.exp2(m_prev - m_new)
        p = jnp.exp2(s - _lane_tile(m_new, tiles))
        l_ref[...] = alpha * l_ref[...] + _lane_fold(p)
        acc_ref[...] = (_lane_tile(alpha, acc_ref.shape[1] // LANES) * acc_ref[...]
                        + jnp.dot(p.astype(BF16), v, preferred_element_type=F32))
        m_ref[...] = m_new
    else:
        p = jnp.exp2(s - _lane_tile(m_ref[...], tiles))
        l_ref[...] += _lane_fold(p)
        acc_ref[...] += jnp.dot(p.astype(BF16), v, preferred_element_type=F32)


def _scan_keys(kt_ref, n_k, v_ref, n_ctx, n_lat, see_latent, step):
    for cc in range(n_ctx // KV_CHUNK):
        step([kt_ref[j, cc] for j in range(n_k)], v_ref[cc * KV_CHUNK:(cc + 1) * KV_CHUNK, :])
    ctx_chunks = n_ctx // KV_CHUNK

    def latent():
        def body(c, carry):
            base = ctx_chunks + c * LAT_SUB
            kts = [jnp.concatenate([kt_ref[j, base + u] for u in range(LAT_SUB)], axis=1) for j in range(n_k)]
            r0 = pl.multiple_of(n_ctx + c * (LAT_SUB * KV_CHUNK), KV_CHUNK)
            step(kts, v_ref[pl.ds(r0, LAT_SUB * KV_CHUNK), :])
            return carry

        lax.fori_loop(0, n_lat, body, 0)

    if see_latent is None:
        latent()
    else:
        pl.when(see_latent)(latent)


def _row_bound(q, kmax):
    qf = q.astype(F32)
    return jnp.sqrt(jnp.sum(qf * qf, axis=-1, keepdims=True)) * kmax


def _gqa_kernel(fast_ref, kmax_ref, q_ref, kt_ref, v_ref, o_ref, qs_ref, m_ref, l_ref, acc_ref,
                *, ctx_tiles, n_ctx, n_lat):
    g = pl.program_id(0)
    tq = q_ref.shape[0]
    group = q_ref.shape[1] // HEAD_DIM
    for u in range(group):
        qs_ref[u * tq:(u + 1) * tq, :] = q_ref[:, u * HEAD_DIM:(u + 1) * HEAD_DIM]
    l_ref[...] = jnp.zeros(l_ref.shape, F32)
    acc_ref[...] = jnp.zeros(acc_ref.shape, F32)
    see_latent = pl.program_id(1) >= ctx_tiles
    fast = fast_ref[g] == 1

    def scan(online):
        def step(kts, v):
            _softmax_step(qs_ref[...], kts[0], v, m_ref, l_ref, acc_ref, online)

        _scan_keys(kt_ref, 1, v_ref, n_ctx, n_lat, see_latent, step)

    @pl.when(fast)
    def _():
        m_ref[...] = jnp.broadcast_to(_row_bound(qs_ref[...], kmax_ref[g]), m_ref.shape)
        scan(False)

    @pl.when(jnp.logical_not(fast))
    def _():
        m_ref[...] = jnp.full(m_ref.shape, NEG_BIG, F32)
        scan(True)

    o = acc_ref[...] / jnp.sum(l_ref[...], axis=-1, keepdims=True)
    for u in range(group):
        o_ref[:, u * HEAD_DIM:(u + 1) * HEAD_DIM] = o[u * tq:(u + 1) * tq, :].astype(o_ref.dtype)


def _kv_chunks(r, n_ctx):
    n_lat_chunks = (r - n_ctx) // KV_CHUNK
    assert n_ctx % KV_CHUNK == 0 and n_lat_chunks % LAT_SUB == 0
    return n_lat_chunks // LAT_SUB


def _gqa_attention(q, qmax, kt, kmax, qkvu, v_col0, n_ctx):
    r = q.shape[0]
    group = A_HEADS // A_KV_HEADS
    tq = Q_TILE_GQA
    nc = r // KV_CHUNK
    vb = v_col0 // HEAD_DIM
    fast = (jnp.max(qmax.reshape(A_KV_HEADS, group), axis=1) * kmax <= FAST_BOUND_MAX).astype(I32)
    kern = functools.partial(_gqa_kernel, ctx_tiles=n_ctx // tq, n_ctx=n_ctx, n_lat=_kv_chunks(r, n_ctx))
    smem = pl.BlockSpec(memory_space=pltpu.SMEM)
    return pl.pallas_call(
        kern,
        grid=(A_KV_HEADS, r // tq),
        in_specs=[
            smem, smem,
            pl.BlockSpec((tq, group * HEAD_DIM), lambda g, i: (i, g)),
            pl.BlockSpec((1, nc, HEAD_DIM, KV_CHUNK), lambda g, i: (g, 0, 0, 0)),
            pl.BlockSpec((r, HEAD_DIM), lambda g, i: (0, vb + g)),
        ],
        out_specs=pl.BlockSpec((tq, group * HEAD_DIM), lambda g, i: (i, g)),
        out_shape=jax.ShapeDtypeStruct((r, A_HEADS * HEAD_DIM), BF16),
        scratch_shapes=[
            pltpu.VMEM((group * tq, HEAD_DIM), BF16),
            pltpu.VMEM((group * tq, LANES), F32),
            pltpu.VMEM((group * tq, LANES), F32),
            pltpu.VMEM((group * tq, HEAD_DIM), F32),
        ],
        compiler_params=_params("arbitrary", "arbitrary"),
        name="gqa_attn",
    )(fast, kmax, q, kt, qkvu)


def _diff_kernel(lam_ref, fast_ref, kmax_ref, q_ref, kt_ref, v_ref, g_ref, o_ref, m0, l0, a0, m1, l1, a1,
                 *, out_scale, n_ctx, n_lat):
    h = pl.program_id(0)
    comps = ((m0, l0, a0), (m1, l1, a1))
    for _, l_ref, a_ref in comps:
        l_ref[...] = jnp.zeros(l_ref.shape, F32)
        a_ref[...] = jnp.zeros(a_ref.shape, F32)
    fast = fast_ref[h] == 1

    def q_comp(c):
        return q_ref[:, c * HEAD_DIM:(c + 1) * HEAD_DIM]

    def scan(online):
        def step(kts, v):
            for c, (m_ref, l_ref, a_ref) in enumerate(comps):
                _softmax_step(q_comp(c), kts[c], v, m_ref, l_ref, a_ref, online)

        _scan_keys(kt_ref, 2, v_ref, n_ctx, n_lat, None, step)

    @pl.when(fast)
    def _():
        for c, (m_ref, _, _) in enumerate(comps):
            m_ref[...] = jnp.broadcast_to(_row_bound(q_comp(c), kmax_ref[2 * h + c]), m_ref.shape)
        scan(False)

    @pl.when(jnp.logical_not(fast))
    def _():
        for m_ref, _, _ in comps:
            m_ref[...] = jnp.full(m_ref.shape, NEG_BIG, F32)
        scan(True)

    o = (a0[...] / jnp.sum(l0[...], axis=-1, keepdims=True)
         - lam_ref[0] * (a1[...] / jnp.sum(l1[...], axis=-1, keepdims=True)))
    o_ref[...] = (_rms(o, g_ref[...]) * out_scale).astype(o_ref.dtype)


def _diff_attention(lam, q, qmax, kt, kmax, qkv, v_col0, subln_g, out_scale, n_ctx):
    r = qkv.shape[0]
    seq = q.shape[0]
    tq = math.gcd(Q_TILE_DIFF, seq)
    nc = r // KV_CHUNK
    dv = 2 * HEAD_DIM
    vb = v_col0 // dv
    fast = (jnp.max((qmax * kmax).reshape(DIFF_HEADS, 2), axis=1) <= FAST_BOUND_MAX).astype(I32)
    kern = functools.partial(_diff_kernel, out_scale=out_scale, n_ctx=n_ctx, n_lat=_kv_chunks(r, n_ctx))
    stat = pltpu.VMEM((tq, LANES), F32)
    acc = pltpu.VMEM((tq, dv), F32)
    smem = pl.BlockSpec(memory_space=pltpu.SMEM)
    return pl.pallas_call(
        kern,
        grid=(DIFF_HEADS, seq // tq),
        in_specs=[
            smem, smem, smem,
            pl.BlockSpec((tq, dv), lambda h, i: (i, h)),
            pl.BlockSpec((2, nc, HEAD_DIM, KV_CHUNK), lambda h, i: (h, 0, 0, 0)),
            pl.BlockSpec((r, dv), lambda h, i: (0, vb + h)),
            pl.BlockSpec((1, dv), lambda h, i: (0, 0)),
        ],
        out_specs=pl.BlockSpec((tq, dv), lambda h, i: (i, h)),
        out_shape=jax.ShapeDtypeStruct((seq, DIFF_HEADS * dv), BF16),
        scratch_shapes=[stat, stat, acc, stat, stat, acc],
        compiler_params=_params("arbitrary", "arbitrary"),
        name="diff_attn",
    )(lam, fast, kmax, q, kt, qkv, subln_g)


def _conv_kernel(ap_ref, gp_ref, ac_ref, gc_ref, an_ref, gn_ref, w_ref, b_ref, lg_ref, lb_ref, o_ref, h_ref, sh_ref,
                 *, seq_starts, seq_ends):
    i = pl.program_id(0)
    tr = ac_ref.shape[0]
    is_start = functools.reduce(jnp.logical_or, [i == s for s in seq_starts])
    is_end = functools.reduce(jnp.logical_or, [i == e for e in seq_ends])

    def glu(a_ref, g_ref):
        return a_ref[...].astype(F32) * _sigmoid(g_ref[...].astype(F32))

    h_ref[0:CONV_HALO, :] = jnp.where(is_start, 0.0, glu(ap_ref, gp_ref))
    h_ref[CONV_HALO:CONV_HALO + tr, :] = glu(ac_ref, gc_ref)
    h_ref[CONV_HALO + tr:, :] = jnp.where(is_end, 0.0, glu(an_ref, gn_ref))
    rows = sh_ref.shape[1]
    for b in range(1, SUBLANES):
        sh_ref[b - 1] = h_ref[b:b + rows, :]
    acc = jnp.zeros((tr, ac_ref.shape[1]), F32)
    base = CONV_HALO - CONV_W // 2
    for j in range(CONV_W):
        a, b = divmod(base + j, SUBLANES)
        r0 = a * SUBLANES
        src = h_ref[r0:r0 + tr, :] if b == 0 else sh_ref[b - 1, r0:r0 + tr, :]
        acc = acc + w_ref[j:j + 1, :] * src
    y = acc + b_ref[...]
    mu = jnp.mean(y, axis=-1, keepdims=True)
    var = jnp.mean(jnp.square(y - mu), axis=-1, keepdims=True)
    z = (y - mu) * lax.rsqrt(var + EPS) * lg_ref[...] + lb_ref[...]
    o_ref[...] = (z * _sigmoid(z)).astype(o_ref.dtype)


CONV_TILE = 128


def _conformer_conv(qkvu, u_col0, c_conv, dw_w, dw_b, ln_g, ln_b, n_ctx):
    r = qkvu.shape[0]
    tr = CONV_TILE
    nt = r // tr
    hb = tr // CONV_HALO
    ab = u_col0 // c_conv
    gb = ab + 1
    last_halo = r // CONV_HALO - 1
    seq_starts = (0, n_ctx // tr)
    seq_ends = (n_ctx // tr - 1, nt - 1)

    def prev(col):
        return pl.BlockSpec((CONV_HALO, c_conv), lambda i: (jnp.maximum(i * hb - 1, 0), col))

    def cur(col):
        return pl.BlockSpec((tr, c_conv), lambda i: (i, col))

    def nxt(col):
        return pl.BlockSpec((CONV_HALO, c_conv), lambda i: (jnp.minimum((i + 1) * hb, last_halo), col))

    vec = pl.BlockSpec((1, c_conv), lambda i: (0, 0))
    return pl.pallas_call(
        functools.partial(_conv_kernel, seq_starts=seq_starts, seq_ends=seq_ends),
        grid=(nt,),
        in_specs=[prev(ab), prev(gb), cur(ab), cur(gb), nxt(ab), nxt(gb),
                  pl.BlockSpec((CONV_W, c_conv), lambda i: (0, 0)), vec, vec, vec],
        out_specs=pl.BlockSpec((tr, c_conv), lambda i: (i, 0)),
        out_shape=jax.ShapeDtypeStruct((r, c_conv), BF16),
        scratch_shapes=[pltpu.VMEM((tr + 2 * CONV_HALO, c_conv), F32),
                        pltpu.VMEM((SUBLANES - 1, tr + 2 * CONV_HALO - SUBLANES, c_conv), F32)],
        compiler_params=_params("arbitrary"),
        name="conformer_conv",
    )(qkvu, qkvu, qkvu, qkvu, qkvu, qkvu, dw_w, dw_b, ln_g, ln_b)


def _post_kernel(*refs, n_in):
    o_refs = refs[:n_in]
    w_ref, h_ref, mod_ref, g_ref, rr_ref, h1_ref, fp_ref, lg_ref = refs[n_in:]
    k0 = 0
    out = None
    for o_in in o_refs:
        kw = o_in.shape[1]
        part = jnp.dot(o_in[...], w_ref[k0:k0 + kw, :], preferred_element_type=F32)
        out = part if out is None else out + part
        k0 += kw
    h1 = h_ref[...] + mod_ref[0, 2:3, :] * _rms(out, g_ref[0:1, :])
    h1_ref[...] = h1
    f = _modulate(h1, g_ref[1:2, :], mod_ref[0, 3:4, :], mod_ref[0, 4:5, :])
    fp_ref[...] = f
    f_hi = f.astype(BF16)
    f_lo = (f - f_hi.astype(F32)).astype(BF16)
    l2 = (jnp.dot(f_hi, rr_ref[...], preferred_element_type=F32)
          + jnp.dot(f_lo, rr_ref[...], preferred_element_type=F32))
    lg_ref[...] = l2 + pltpu.roll(l2, N_EXPERTS, axis=1)


def _post(o_parts, w_out, h, row_off, mod, g12, rr, n_ctx):
    n = o_parts[0].shape[0]
    d = h.shape[1]
    tm = ROW_TILE
    off = row_off // tm
    ctx_tiles = (n_ctx - row_off) // tm
    in_specs = [pl.BlockSpec((tm, o.shape[1]), lambda i: (i, 0)) for o in o_parts]
    in_specs += [
        pl.BlockSpec((d, d), lambda i: (0, 0)),
        pl.BlockSpec((tm, d), lambda i: (i + off, 0)),
        pl.BlockSpec((1, 6, d), lambda i: (jnp.where(i < ctx_tiles, 1, 0), 0, 0)),
        pl.BlockSpec((2, d), lambda i: (0, 0)),
        pl.BlockSpec((d, 2 * N_EXPERTS), lambda i: (0, 0)),
    ]
    return pl.pallas_call(
        functools.partial(_post_kernel, n_in=len(o_parts)),
        grid=(n // tm,),
        in_specs=in_specs,
        out_specs=[
            pl.BlockSpec((tm, d), lambda i: (i, 0)),
            pl.BlockSpec((tm, d), lambda i: (i, 0)),
            pl.BlockSpec((tm, 2 * N_EXPERTS), lambda i: (i, 0)),
        ],
        out_shape=[
            jax.ShapeDtypeStruct((n, d), F32),
            jax.ShapeDtypeStruct((n, d), F32),
            jax.ShapeDtypeStruct((n, 2 * N_EXPERTS), F32),
        ],
        compiler_params=_params("arbitrary"),
        name="post_mixer",
    )(*o_parts, w_out, h, mod, g12, rr)


def _route_kernel(lg_ref, bias_ref, e_ref, w_ref, rank_ref, cnt_ref):
    t = lg_ref.shape[0]
    gsz = N_EXPERTS // N_GROUPS

    @pl.when(pl.program_id(0) == 0)
    def _():
        cnt_ref[...] = jnp.zeros(cnt_ref.shape, F32)

    logits = lg_ref[...].T[0:N_EXPERTS, :]
    scores = _sigmoid(logits)
    choice = scores + bias_ref[...]
    sub = lax.broadcasted_iota(I32, (gsz, t), 0).astype(F32)
    grp_rows = []
    for g in range(N_GROUPS):
        cg = choice[g * gsz:(g + 1) * gsz, :]
        m1 = jnp.max(cg, axis=0, keepdims=True)
        i1 = jnp.min(jnp.where(cg == m1, sub, float(gsz)), axis=0, keepdims=True)
        m2 = jnp.max(jnp.where(sub == i1, -jnp.inf, cg), axis=0, keepdims=True)
        grp_rows.append(m1 + m2)
    grp = jnp.concatenate(grp_rows, axis=0)
    gidx = lax.broadcasted_iota(I32, (N_GROUPS, t), 0).astype(F32)
    gsel = jnp.zeros((N_GROUPS, t), F32)
    for _ in range(TOPK_GROUPS):
        gm = jnp.max(grp, axis=0, keepdims=True)
        gi = jnp.min(jnp.where(grp == gm, gidx, float(N_GROUPS)), axis=0, keepdims=True)
        hit = gidx == gi
        gsel = jnp.where(hit, 1.0, gsel)
        grp = jnp.where(hit, -jnp.inf, grp)
    eidx = lax.broadcasted_iota(I32, (N_EXPERTS, t), 0).astype(F32)
    mc_rows = []
    for g in range(N_GROUPS):
        mc_rows.append(jnp.where(gsel[g:g + 1, :] > 0.5, choice[g * gsz:(g + 1) * gsz, :], -jnp.inf))
    mc = jnp.concatenate(mc_rows, axis=0)
    sel_all = jnp.zeros((N_EXPERTS, t), F32)
    e_rows, w_rows = [], []
    for _ in range(TOP_K):
        mx = jnp.max(mc, axis=0, keepdims=True)
        ei = jnp.min(jnp.where(mc == mx, eidx, float(N_EXPERTS)), axis=0, keepdims=True)
        hit = eidx == ei
        e_rows.append(ei)
        w_rows.append(jnp.sum(jnp.where(hit, scores, 0.0), axis=0, keepdims=True))
        sel_all = jnp.where(hit, 1.0, sel_all)
        mc = jnp.where(hit, -jnp.inf, mc)
    e_all = jnp.concatenate(e_rows, axis=0)
    w_all = jnp.concatenate(w_rows, axis=0)
    w_ref[...] = w_all / jnp.sum(w_all, axis=0, keepdims=True) * ROUTED_SCALE
    e_ref[...] = e_all.astype(I32)
    src = lax.broadcasted_iota(I32, (t, t), 0)
    dst = lax.broadcasted_iota(I32, (t, t), 1)
    upper = jnp.where(src < dst, 1.0, 0.0).astype(BF16)
    pos = jnp.dot(sel_all.astype(BF16), upper, preferred_element_type=F32) + cnt_ref[:, 0:1]
    r_rows = [jnp.sum(jnp.where(eidx == e_rows[k], pos, 0.0), axis=0, keepdims=True) for k in range(TOP_K)]
    rank_ref[...] = jnp.concatenate(r_rows, axis=0).astype(I32)
    cnt_ref[...] = cnt_ref[...] + jnp.sum(sel_all, axis=1, keepdims=True)


def _route(logits, bias):
    n = logits.shape[0]
    t = ROUTE_TILE
    row = pl.BlockSpec((TOP_K, t), lambda i: (0, i))
    return pl.pallas_call(
        _route_kernel,
        grid=(n // t,),
        in_specs=[
            pl.BlockSpec((t, 2 * N_EXPERTS), lambda i: (i, 0)),
            pl.BlockSpec((N_EXPERTS, 1), lambda i: (0, 0)),
        ],
        out_specs=[row, row, row, pl.BlockSpec((N_EXPERTS, LANES), lambda i: (0, 0))],
        out_shape=[
            jax.ShapeDtypeStruct((TOP_K, n), I32),
            jax.ShapeDtypeStruct((TOP_K, n), F32),
            jax.ShapeDtypeStruct((TOP_K, n), I32),
            jax.ShapeDtypeStruct((N_EXPERTS, LANES), F32),
        ],
        compiler_params=_params("arbitrary"),
        name="route",
    )(logits, bias)


def _dispatch_kernel(meta_ref, nt_ref, slot_hbm, fp_ref, xs_hbm, slot_smem, zero_ref, sem_idx, sem_rows, sem_zero):
    i = pl.program_id(0)
    t = fp_ref.shape[0]
    idx_copy = pltpu.make_async_copy(slot_hbm.at[i], slot_smem, sem_idx)
    idx_copy.start()
    idx_copy.wait()

    def issue(tok, carry):
        for k in range(TOP_K):
            s = slot_smem[k * t + tok]
            pltpu.make_async_copy(fp_ref.at[pl.ds(tok, 1), :], xs_hbm.at[pl.ds(s, 1), :],
                                  sem_rows).start(priority=k % 2)
        return carry

    lax.fori_loop(0, t, issue, 0)
    for _ in range(TOP_K):
        pltpu.make_async_copy(fp_ref, xs_hbm.at[pl.ds(0, t), :], sem_rows).wait()

    @pl.when(i == pl.num_programs(0) - 1)
    def _():
        tm = zero_ref.shape[0]
        n_all = xs_hbm.shape[0] // tm
        zero_ref[...] = jnp.zeros(zero_ref.shape, zero_ref.dtype)

        def pad_row(row):
            return pltpu.make_async_copy(zero_ref.at[pl.ds(0, 1), :], xs_hbm.at[pl.ds(row, 1), :], sem_zero)

        def tail_tile(b):
            return pltpu.make_async_copy(zero_ref, xs_hbm.at[pl.ds(pl.multiple_of(b * tm, tm), tm), :], sem_zero)

        def per_expert(e, total):
            first = meta_ref[0, e] + meta_ref[1, e]
            n_pad = meta_ref[2, e] - meta_ref[1, e]

            def one(r, carry):
                pad_row(first + r).start()
                return carry

            lax.fori_loop(0, n_pad, one, 0)
            return total + n_pad

        total = lax.fori_loop(0, N_EXPERTS, per_expert, 0)

        def start_tail(b, carry):
            tail_tile(b).start()
            return carry

        lax.fori_loop(nt_ref[0], n_all, start_tail, 0)

        def wait_row(r, carry):
            pad_row(0).wait()
            return carry

        lax.fori_loop(0, total, wait_row, 0)

        def wait_tail(b, carry):
            tail_tile(b).wait()
            return carry

        lax.fori_loop(nt_ref[0], n_all, wait_tail, 0)


def _dispatch(meta, n_tiles, slot_tiles, fp, cap):
    n, c = fp.shape
    t = ROUTE_TILE
    smem = pl.BlockSpec(memory_space=pltpu.SMEM)
    return pl.pallas_call(
        _dispatch_kernel,
        grid=(n // t,),
        in_specs=[
            smem, smem,
            pl.BlockSpec(memory_space=pl.ANY),
            pl.BlockSpec((t, c), lambda i: (i, 0)),
        ],
        out_specs=pl.BlockSpec(memory_space=pl.ANY),
        out_shape=jax.ShapeDtypeStruct((cap, c), F32),
        scratch_shapes=[
            pltpu.SMEM((TOP_K * t,), I32),
            pltpu.VMEM((EXPERT_TILE, c), F32),
            pltpu.SemaphoreType.DMA, pltpu.SemaphoreType.DMA, pltpu.SemaphoreType.DMA,
        ],
        compiler_params=_params("arbitrary"),
        name="dispatch",
    )(meta, n_tiles, slot_tiles, fp)


def _experts_kernel(be_ref, nt_ref, xs_ref, wg_ref, wu_ref, wd_ref, ys_ref):
    del be_ref
    used = pl.program_id(0) < nt_ref[0]

    @pl.when(jnp.logical_not(used))
    def _():
        ys_ref[...] = jnp.zeros(ys_ref.shape, ys_ref.dtype)

    @pl.when(used)
    def _():
        x = xs_ref[...].astype(BF16)

        def proj(w_ref):
            return jnp.dot(x, w_ref[0, 0].astype(BF16), preferred_element_type=F32)

        g = proj(wg_ref)
        u = proj(wu_ref)
        a = (g * _sigmoid(g) * u).astype(BF16)
        ys_ref[...] = jnp.dot(a, wd_ref[0, 0].astype(BF16), preferred_element_type=F32)


def _experts(layer, block_e, n_tiles, xs, wg, wu, wd):
    cap, c = xs.shape
    _, e, d, ff = wg.shape
    tm = EXPERT_TILE

    def row_map(b, be, nt):
        return (jnp.minimum(b, nt[0] - 1), 0)

    grid_spec = pltpu.PrefetchScalarGridSpec(
        num_scalar_prefetch=2,
        grid=(cap // tm,),
        in_specs=[
            pl.BlockSpec((tm, c), row_map),
            pl.BlockSpec((1, 1, d, ff), lambda b, be, nt: (layer, be[b], 0, 0)),
            pl.BlockSpec((1, 1, d, ff), lambda b, be, nt: (layer, be[b], 0, 0)),
            pl.BlockSpec((1, 1, ff, d), lambda b, be, nt: (layer, be[b], 0, 0)),
        ],
        out_specs=pl.BlockSpec((tm, c), lambda b, be, nt: (b, 0)),
    )
    return pl.pallas_call(
        _experts_kernel,
        grid_spec=grid_spec,
        out_shape=jax.ShapeDtypeStruct((cap, c), F32),
        compiler_params=_params("arbitrary"),
        name="experts",
    )(block_e, n_tiles, xs, wg, wu, wd)


def _combine_kernel(slot_hbm, h_ref, w_ref, mod_ref, g_ref, sg_ref, su_ref, sd_ref, ys_hbm, o_ref,
                    slot_smem, ybuf, sem_idx, sem_rows):
    i = pl.program_id(0)
    t = h_ref.shape[0]

    def start_gather(tile, buf):
        base = buf * (TOP_K * t)
        idx_copy = pltpu.make_async_copy(slot_hbm.at[tile], slot_smem.at[pl.ds(base, TOP_K * t)], sem_idx)
        idx_copy.start()
        idx_copy.wait()

        def issue(tok, carry):
            for k in range(TOP_K):
                s = slot_smem[base + k * t + tok]
                pltpu.make_async_copy(ys_hbm.at[pl.ds(s, 1), :], ybuf.at[buf, k, pl.ds(tok, 1), :],
                                      sem_rows.at[buf]).start(priority=k % 2)
            return carry

        lax.fori_loop(0, t, issue, 0)

    @pl.when(i == 0)
    def _():
        start_gather(0, 0)

    cur = i % 2
    for buf in (0, 1):
        @pl.when(jnp.logical_and(i + 1 < pl.num_programs(0), cur != buf))
        def _():
            start_gather(i + 1, buf)

    h1 = h_ref[...]
    f = _modulate(h1, g_ref[0:1, :], mod_ref[0, 3:4, :], mod_ref[0, 4:5, :]).astype(BF16)
    sg = jnp.dot(f, sg_ref[...], preferred_element_type=F32)
    su = jnp.dot(f, su_ref[...], preferred_element_type=F32)
    y = jnp.dot((sg * _sigmoid(sg) * su).astype(BF16), sd_ref[...], preferred_element_type=F32)
    for k in range(TOP_K):
        pltpu.make_async_copy(ys_hbm.at[pl.ds(0, t), :], ybuf.at[cur, k], sem_rows.at[cur]).wait()
    wt = w_ref[...].T
    for k in range(TOP_K):
        y = y + wt[:, k:k + 1] * ybuf[cur, k]
    o_ref[...] = h1 + mod_ref[0, 5:6, :] * _rms(y, g_ref[1:2, :])


def _combine(slot_tiles, h1, w, mod, g23, sg, su, sd, ys, n_ctx):
    n, d = h1.shape
    t = ROUTE_TILE
    c = ys.shape[1]
    ff = sg.shape[1]
    ctx_tiles = n_ctx // t
    return pl.pallas_call(
        _combine_kernel,
        grid=(n // t,),
        in_specs=[
            pl.BlockSpec(memory_space=pl.ANY),
            pl.BlockSpec((t, d), lambda i: (i, 0)),
            pl.BlockSpec((TOP_K, t), lambda i: (0, i)),
            pl.BlockSpec((1, 6, d), lambda i: (jnp.where(i < ctx_tiles, 1, 0), 0, 0)),
            pl.BlockSpec((2, d), lambda i: (0, 0)),
            pl.BlockSpec((d, ff), lambda i: (0, 0)),
            pl.BlockSpec((d, ff), lambda i: (0, 0)),
            pl.BlockSpec((ff, d), lambda i: (0, 0)),
            pl.BlockSpec(memory_space=pl.ANY),
        ],
        out_specs=pl.BlockSpec((t, d), lambda i: (i, 0)),
        out_shape=jax.ShapeDtypeStruct((n, d), F32),
        scratch_shapes=[
            pltpu.SMEM((2 * TOP_K * t,), I32),
            pltpu.VMEM((2, TOP_K, t, c), F32),
            pltpu.SemaphoreType.DMA,
            pltpu.SemaphoreType.DMA((2,)),
        ],
        compiler_params=_params("arbitrary"),
        name="combine",
    )(slot_tiles, h1, w, mod, g23, sg, su, sd, ys)


def _rope_tables(n_ctx, seq):
    nf = HEAD_DIM // 4
    rows = seq // GRID_W
    inv = ROPE_THETA ** (-jnp.arange(nf, dtype=F32) / nf)
    row_ang = jnp.arange(rows, dtype=F32)[:, None] * inv
    col_ang = jnp.arange(GRID_W, dtype=F32)[:, None] * inv

    def per_token(row_tab, col_tab):
        return jnp.repeat(row_tab, GRID_W, axis=0), jnp.tile(col_tab, (rows, 1))

    cos_r, cos_c = per_token(jnp.cos(row_ang), jnp.cos(col_ang))
    sin_r, sin_c = per_token(jnp.sin(row_ang), jnp.sin(col_ang))
    cos = jnp.concatenate([cos_r, cos_c, cos_r, cos_c], axis=1)
    sin = jnp.concatenate([-sin_r, -sin_c, sin_r, sin_c], axis=1)
    cos = jnp.concatenate([jnp.ones((n_ctx, HEAD_DIM), F32), cos], axis=0)
    sin = jnp.concatenate([jnp.zeros((n_ctx, HEAD_DIM), F32), sin], axis=0)
    return cos, sin


def _head_lane_order(x):
    y = x.reshape(x.shape[:-1] + (x.shape[-1] // HEAD_DIM, 2, 2, HEAD_DIM // 4))
    return jnp.swapaxes(y, -3, -2).reshape(x.shape)


def _split_router(router):
    hi = router.astype(BF16)
    lo = (router - hi.astype(F32)).astype(BF16)
    return jnp.concatenate([hi, lo], axis=1)


def _moe(layer, h1, fp, logits, mod, g23, bias, wg, wu, wd, sg, su, sd, n_ctx):
    n = h1.shape[0]
    t = ROUTE_TILE
    tm = EXPERT_TILE
    top_e, w, rank, cnt = _route(logits, bias.reshape(N_EXPERTS, 1))
    counts = cnt[:, 0].astype(I32)
    padded = (counts + tm - 1) // tm * tm
    pad_end = jnp.cumsum(padded)
    pad_start = pad_end - padded
    onehot = top_e[:, :, None] == jnp.arange(N_EXPERTS, dtype=I32)
    slot = rank + jnp.sum(jnp.where(onehot, pad_start, 0), axis=-1)
    slot_tiles = slot.reshape(TOP_K, n // t, t).transpose(1, 0, 2).reshape(n // t, TOP_K * t)
    cap = n * TOP_K + N_EXPERTS * tm
    nb = cap // tm
    n_tiles = (pad_end[-1] // tm).astype(I32).reshape(1)
    tile_row = jnp.minimum(jnp.arange(nb, dtype=I32), n_tiles[0] - 1) * tm
    block_e = jnp.sum((pad_end[None, :] <= tile_row[:, None]).astype(I32), axis=1)
    block_e = jnp.minimum(block_e, N_EXPERTS - 1).astype(I32)
    meta = jnp.stack([pad_start, counts, padded]).astype(I32)
    xs = _dispatch(meta, n_tiles, slot_tiles, fp, cap)
    ys = _experts(layer, block_e, n_tiles, xs, wg, wu, wd)
    return _combine(slot_tiles, h1, w, mod, g23, sg, su, sd, ys, n_ctx)


def kernel(x, c, ctx, c_ctx, ada_w, ada_b, norm_g, ev_w_in, ev_w_out, ev_q_norm, ev_k_norm, ev_dw_w, ev_dw_b,
           ev_ln_g, ev_ln_b, od_w_in, od_w_out, od_lambda, od_subln_g, moe_router, moe_bias, moe_w_gate,
           moe_w_up, moe_w_down, sh_w_gate, sh_w_up, sh_w_down):
    b, seq, d = x.shape
    n_ctx = ctx.shape[1]
    assert b == 1 and n_ctx % ROW_TILE == 0 and seq % ROW_TILE == 0 and seq % GRID_W == 0
    depth = ada_w.shape[0]
    assert depth == 2
    cos, sin = _rope_tables(n_ctx, seq)
    cvec = jnp.stack([c[0], c_ctx], axis=1)
    mod_all = _ada(cvec, ada_w, ada_b).reshape(depth, 2, 6, d)
    h = jnp.concatenate([ctx[0], x[0]], axis=0)
    qscale = math.log2(math.e) * HEAD_DIM ** -0.5
    c_conv = ev_dw_w.shape[-1]
    wg, wu, wd = moe_w_gate, moe_w_up, moe_w_down
    sg, su, sd = sh_w_gate.astype(BF16), sh_w_up.astype(BF16), sh_w_down.astype(BF16)

    mod = mod_all[0]
    gains = norm_g[0]
    nq = A_HEADS * HEAD_DIM
    nkv = 2 * A_KV_HEADS * HEAD_DIM
    nk = A_KV_HEADS * HEAD_DIM
    w0 = ev_w_in[0]
    w_in = jnp.concatenate([_head_lane_order(w0[:, :nq]), w0[:, nq + nkv:],
                            _head_lane_order(w0[:, nq:nq + nk]), w0[:, nq + nk:nq + nkv]], axis=1)
    qkvu = _modmm(h, mod, gains[0:1], w_in.astype(BF16), n_ctx, col_tile=w_in.shape[1] // 2)
    u_col0 = nq
    k_col0 = nq + 2 * c_conv
    v_col0 = k_col0 + nk
    q_gain = _head_lane_order(ev_q_norm[0].reshape(1, HEAD_DIM))
    k_gain = _head_lane_order(ev_k_norm[0].reshape(1, HEAD_DIM))
    q, qmax = _prep(qkvu, 0, 0, A_HEADS, cos, sin, q_gain, True, qscale, False)
    kt, kmax = _prep(qkvu, 0, k_col0, A_KV_HEADS, cos, sin, k_gain, True, 1.0, True)
    o_attn = _gqa_attention(q, qmax, kt, kmax, qkvu, v_col0, n_ctx)
    o_conv = _conformer_conv(qkvu, u_col0, c_conv, ev_dw_w[0], ev_dw_b[0].reshape(1, -1),
                             ev_ln_g[0].reshape(1, -1), ev_ln_b[0].reshape(1, -1), n_ctx)
    h1, fp, logits = _post([o_attn, o_conv], ev_w_out[0].astype(BF16), h, 0, mod, gains[1:3],
                           _split_router(moe_router[0]), n_ctx)
    h = _moe(0, h1, fp, logits, mod, gains[2:4], moe_bias[0], wg, wu, wd, sg[0], su[0], sd[0], n_ctx)

    mod = mod_all[1]
    gains = norm_g[1]
    lam_init = 0.8 - 0.6 * math.exp(-0.3 * 1)
    lp = od_lambda[0].astype(F32)
    lam = (jnp.exp(jnp.sum(lp[0] * lp[1])) - jnp.exp(jnp.sum(lp[2] * lp[3])) + lam_init).reshape(1)
    n_qk = 2 * DIFF_HEADS
    w1 = od_w_in[0]
    w1 = jnp.concatenate([_head_lane_order(w1[:, :2 * n_qk * HEAD_DIM]), w1[:, 2 * n_qk * HEAD_DIM:]], axis=1)
    qkv = _modmm(h, mod, gains[0:1], w1.astype(BF16), n_ctx, col_tile=w1.shape[1] // 4)
    ones = jnp.ones((1, HEAD_DIM), F32)
    q, qmax = _prep(qkv, n_ctx, 0, n_qk, cos, sin, ones, False, qscale, False)
    kt, kmax = _prep(qkv, 0, n_qk * HEAD_DIM, n_qk, cos, sin, ones, False, 1.0, True)
    o_attn = _diff_attention(lam, q, qmax, kt, kmax, qkv, 2 * n_qk * HEAD_DIM, od_subln_g[0].reshape(1, -1),
                             1.0 - lam_init, n_ctx)
    h1, fp, logits = _post([o_attn], od_w_out[0].astype(BF16), h, n_ctx, mod, gains[1:3],
                           _split_router(moe_router[1]), n_ctx)
    h = _moe(1, h1, fp, logits, mod, gains[2:4], moe_bias[1], wg, wu, wd, sg[1], su[1], sd[1], 0)
    return h[None]
```
